```python
import math
import jax, jax.numpy as jnp
from jax import lax
import numpy as np

D_MODEL = 1024
BATCH = 8
SEQ = 4096
DEPTH = 2

CHUNK = 64
N_A_LAYERS = DEPTH // 2
N_B_LAYERS = DEPTH - N_A_LAYERS
N_DENSE = (DEPTH + 1) // 2
N_MOE = DEPTH // 2
SSM_GROUP = 16
SSM_GROUPS = D_MODEL // SSM_GROUP
SSM_STATE = 64
DT_MIN = 1e-3
DT_MAX = 1e-1
N_HEADS = 16
QK_NOPE = 64
QK_ROPE = 32
V_HEAD = 64
Q_LORA = 512
KV_LORA = 256
ROPE_BASE = 10000.0
Q_BLOCK = 128
D_FF = 2688
N_EXPERTS = 8
TOP_K = 2
MOE_FF = 3584
EPS = 1e-6

kernel_name = "yoco_s5_mla_moe_hybrid"

F32 = jnp.float32


def rmsnorm(x, g):
    xf = x.astype(F32)
    y = xf * lax.rsqrt(jnp.mean(xf * xf, axis=-1, keepdims=True) + EPS)
    return (y * g.astype(F32)).astype(x.dtype)


def apply_rope(x, cos, sin):
    x1, x2 = jnp.split(x.astype(F32), 2, axis=-1)
    return jnp.concatenate([x1 * cos - x2 * sin, x2 * cos + x1 * sin], axis=-1).astype(x.dtype)


def swiglu(h, w_gate, w_up, w_down):
    return (jax.nn.silu(h @ w_gate) * (h @ w_up)) @ w_down


def s5_mixer(h, w_in, lam_re, lam_im, log_dt, b_re, b_im, c_re, c_im, d_skip, w_glu, w_out):
    bsz, seq, _ = h.shape
    u = (h @ w_in).astype(F32).reshape(bsz, seq, SSM_GROUPS, SSM_GROUP)
    dt = jnp.exp(log_dt.astype(F32))[:, None]
    lr, li = lam_re.astype(F32), lam_im.astype(F32)
    mag = jnp.exp(lr * dt)
    ab_re, ab_im = mag * jnp.cos(li * dt), mag * jnp.sin(li * dt)
    den = lr * lr + li * li
    nr, ni = ab_re - 1.0, ab_im
    coef_re = (nr * lr + ni * li) / den
    coef_im = (ni * lr - nr * li) / den
    br, bi = b_re.astype(F32), b_im.astype(F32)
    bb_re = coef_re[..., None] * br - coef_im[..., None] * bi
    bb_im = coef_re[..., None] * bi + coef_im[..., None] * br
    bu_re = jnp.einsum('blgn,gpn->blgp', u, bb_re)
    bu_im = jnp.einsum('blgn,gpn->blgp', u, bb_im)
    a_re = jnp.broadcast_to(ab_re[None, None], (1, seq, SSM_GROUPS, SSM_STATE))
    a_im = jnp.broadcast_to(ab_im[None, None], (1, seq, SSM_GROUPS, SSM_STATE))

    def combine(e1, e2):
        a1r, a1i, b1r, b1i = e1
        a2r, a2i, b2r, b2i = e2
        return (a2r * a1r - a2i * a1i,
                a2r * a1i + a2i * a1r,
                a2r * b1r - a2i * b1i + b2r,
                a2r * b1i + a2i * b1r + b2i)

    _, _, s_re, s_im = lax.associative_scan(combine, (a_re, a_im, bu_re, bu_im), axis=1)
    y = (jnp.einsum('blgp,gnp->blgn', s_re, c_re.astype(F32))
         - jnp.einsum('blgp,gnp->blgn', s_im, c_im.astype(F32)))
    y = y.reshape(bsz, seq, D_MODEL) + d_skip.astype(F32) * u.reshape(bsz, seq, D_MODEL)
    y = jax.nn.gelu(y).astype(h.dtype)
    z = y * jax.nn.sigmoid(y @ w_glu)
    return z @ w_out


def shared_kv(h, w_dkv, kv_latent_norm, w_ukv, cos, sin):
    bsz, seq, _ = h.shape
    ckr = h @ w_dkv
    c_kv = rmsnorm(ckr[..., :KV_LORA], kv_latent_norm)
    k_rope = apply_rope(ckr[..., KV_LORA:], cos, sin)
    kv = (c_kv @ w_ukv).reshape(bsz, seq, N_HEADS, QK_NOPE + V_HEAD)
    return kv[..., :QK_NOPE], k_rope, kv[..., QK_NOPE:]


def mla_mixer(h, k_nope, k_rope, v, w_dq, q_latent_norm, w_uq, w_o, cos, sin):
    bsz, seq, _ = h.shape
    cq = rmsnorm(h @ w_dq, q_latent_norm)
    q = (cq @ w_uq).reshape(bsz, seq, N_HEADS, QK_NOPE + QK_ROPE)
    scale = (QK_NOPE + QK_ROPE) ** -0.5
    q_nope = q[..., :QK_NOPE] * scale
    q_rope = apply_rope(q[..., QK_NOPE:], cos[:, :, None], sin[:, :, None]) * scale
    n_blk = seq // Q_BLOCK
    qn_b = q_nope.reshape(bsz, n_blk, Q_BLOCK, N_HEADS, QK_NOPE).transpose(1, 0, 2, 3, 4)
    qr_b = q_rope.reshape(bsz, n_blk, Q_BLOCK, N_HEADS, QK_ROPE).transpose(1, 0, 2, 3, 4)
    k_chunk = jnp.arange(seq) // CHUNK

    def block(args):
        i, qn, qr = args
        s = (jnp.einsum('bqhd,bkhd->bhqk', qn, k_nope, preferred_element_type=F32)
             + jnp.einsum('bqhr,bkr->bhqk', qr, k_rope, preferred_element_type=F32))
        q_chunk = (i * Q_BLOCK + jnp.arange(Q_BLOCK)) // CHUNK
        mask = k_chunk[None, :] <= q_chunk[:, None]
        p = jax.nn.softmax(jnp.where(mask, s, -jnp.inf), axis=-1)
        return jnp.einsum('bhqk,bkhd->bqhd', p.astype(v.dtype), v)

    o = lax.map(block, (jnp.arange(n_blk), qn_b, qr_b))
    o = o.transpose(1, 0, 2, 3, 4).reshape(bsz, seq, N_HEADS * V_HEAD)
    return o @ w_o


def moe_swiglu(h, router_w, w_gate, w_up, w_down):
    bsz, seq, d = h.shape
    t = h.reshape(-1, d)
    logits = (t @ router_w).astype(F32)
    top_logit, top_idx = lax.top_k(logits, TOP_K)
    top_w = jax.nn.softmax(top_logit, axis=-1)
    gates = jnp.einsum('nk,nke->ne', top_w, jax.nn.one_hot(top_idx, N_EXPERTS, dtype=F32))
    out = jnp.zeros_like(t)
    for e in range(N_EXPERTS):
        out = out + gates[:, e:e + 1].astype(t.dtype) * swiglu(t, w_gate[e], w_up[e], w_down[e])
    return out.reshape(bsz, seq, d)


def setup_inputs(seed: int = 0) -> dict:
    key = jax.random.key(seed)
    ks = iter(jax.random.split(key, 40))

    def nrm(shape, fan_in):
        return jax.random.normal(next(ks), shape, F32) * (fan_in ** -0.5)

    def gain(shape):
        return 1.0 + 0.02 * jax.random.normal(next(ks), shape, F32)

    x = jax.random.normal(next(ks), (BATCH, SEQ, D_MODEL), F32)
    offsets = jax.random.randint(next(ks), (BATCH, 1), 0, 1024, dtype=jnp.int32)
    positions = offsets + jnp.arange(SEQ, dtype=jnp.int32)[None, :]
    n_idx = jnp.arange(SSM_STATE, dtype=F32)
    gp = (N_A_LAYERS, SSM_GROUPS, SSM_STATE)
    s5_lambda_re = -0.5 + 0.01 * jax.random.normal(next(ks), gp, F32)
    s5_lambda_im = math.pi * n_idx + 0.01 * jax.random.normal(next(ks), gp, F32)
    s5_log_dt = jax.random.uniform(next(ks), (N_A_LAYERS, SSM_GROUPS), F32,
                                   math.log(DT_MIN), math.log(DT_MAX))
    bshape = (N_A_LAYERS, SSM_GROUPS, SSM_STATE, SSM_GROUP)
    cshape = (N_A_LAYERS, SSM_GROUPS, SSM_GROUP, SSM_STATE)
    return {
        "x": x,
        "positions": positions,
        "norm_mix": gain((DEPTH, D_MODEL)),
        "norm_ffn": gain((DEPTH, D_MODEL)),
        "final_norm": gain((D_MODEL,)),
        "s5_w_in": nrm((N_A_LAYERS, D_MODEL, D_MODEL), D_MODEL),
        "s5_lambda_re": s5_lambda_re,
        "s5_lambda_im": s5_lambda_im,
        "s5_log_dt": s5_log_dt,
        "s5_b_re": nrm(bshape, 2 * SSM_GROUP),
        "s5_b_im": nrm(bshape, 2 * SSM_GROUP),
        "s5_c_re": nrm(cshape, SSM_STATE),
        "s5_c_im": nrm(cshape, SSM_STATE),
        "s5_d": gain((N_A_LAYERS, D_MODEL)),
        "s5_w_glu": nrm((N_A_LAYERS, D_MODEL, D_MODEL), D_MODEL),
        "s5_w_out": nrm((N_A_LAYERS, D_MODEL, D_MODEL), D_MODEL),
        "kv_norm": gain((D_MODEL,)),
        "w_dkv": nrm((D_MODEL, KV_LORA + QK_ROPE), D_MODEL),
        "kv_latent_norm": gain((KV_LORA,)),
        "w_ukv": nrm((KV_LORA, N_HEADS * (QK_NOPE + V_HEAD)), KV_LORA),
        "w_dq": nrm((N_B_LAYERS, D_MODEL, Q_LORA), D_MODEL),
        "q_latent_norm": gain((N_B_LAYERS, Q_LORA)),
        "w_uq": nrm((N_B_LAYERS, Q_LORA, N_HEADS * (QK_NOPE + QK_ROPE)), Q_LORA),
        "w_o": nrm((N_B_LAYERS, N_HEADS * V_HEAD, D_MODEL), N_HEADS * V_HEAD),
        "ffn_w_gate": nrm((N_DENSE, D_MODEL, D_FF), D_MODEL),
        "ffn_w_up": nrm((N_DENSE, D_MODEL, D_FF), D_MODEL),
        "ffn_w_down": nrm((N_DENSE, D_FF, D_MODEL), D_FF),
        "router_w": nrm((N_MOE, D_MODEL, N_EXPERTS), D_MODEL),
        "moe_w_gate": nrm((N_MOE, N_EXPERTS, D_MODEL, MOE_FF), D_MODEL),
        "moe_w_up": nrm((N_MOE, N_EXPERTS, D_MODEL, MOE_FF), D_MODEL),
        "moe_w_down": nrm((N_MOE, N_EXPERTS, MOE_FF, D_MODEL), MOE_FF),
    }


def reference(x, positions, norm_mix, norm_ffn, final_norm, s5_w_in, s5_lambda_re, s5_lambda_im,
              s5_log_dt, s5_b_re, s5_b_im, s5_c_re, s5_c_im, s5_d, s5_w_glu, s5_w_out,
              kv_norm, w_dkv, kv_latent_norm, w_ukv, w_dq, q_latent_norm, w_uq, w_o,
              ffn_w_gate, ffn_w_up, ffn_w_down, router_w, moe_w_gate, moe_w_up, moe_w_down):
    half = QK_ROPE // 2
    inv_freq = ROPE_BASE ** (-jnp.arange(half, dtype=F32) * (2.0 / QK_ROPE))
    ang = positions.astype(F32)[..., None] * inv_freq
    cos, sin = jnp.cos(ang), jnp.sin(ang)
    kv = None
    for i in range(DEPTH):
        if i < N_A_LAYERS:
            a = i
            h = rmsnorm(x, norm_mix[i])
            x = x + s5_mixer(h, s5_w_in[a], s5_lambda_re[a], s5_lambda_im[a], s5_log_dt[a],
                             s5_b_re[a], s5_b_im[a], s5_c_re[a], s5_c_im[a], s5_d[a],
                             s5_w_glu[a], s5_w_out[a])
        else:
            if kv is None:
                kv = shared_kv(rmsnorm(x, kv_norm), w_dkv, kv_latent_norm, w_ukv, cos, sin)
            b = i - N_A_LAYERS
            k_nope, k_rope, v = kv
            h = rmsnorm(x, norm_mix[i])
            x = x + mla_mixer(h, k_nope, k_rope, v, w_dq[b], q_latent_norm[b], w_uq[b],
                              w_o[b], cos, sin)
        h = rmsnorm(x, norm_ffn[i])
        if i % 2 == 0:
            j = i // 2
            x = x + swiglu(h, ffn_w_gate[j], ffn_w_up[j], ffn_w_down[j])
        else:
            j = i // 2
            x = x + moe_swiglu(h, router_w[j], moe_w_gate[j], moe_w_up[j], moe_w_down[j])
    return rmsnorm(x, final_norm)
```

```python
import functools

import jax
import jax.numpy as jnp
from jax import lax
from jax.experimental import pallas as pl
from jax.experimental.pallas import tpu as pltpu

F32 = jnp.float32
BF16 = jnp.bfloat16
I32 = jnp.int32
HIGHEST = lax.Precision.HIGHEST

D_MODEL = 1024
CHUNK = 64
SSM_GROUP = 16
SSM_GROUPS = D_MODEL // SSM_GROUP
SSM_STATE = 64
N_HEADS = 16
QK_NOPE = 64
QK_ROPE = 32
V_HEAD = 64
Q_LORA = 512
KV_LORA = 256
ROPE_BASE = 10000.0
D_FF = 2688
N_EXPERTS = 8
MOE_FF = 3584
EPS = 1e-6

LANES = 128
SUBLANES = 8
HEAD_PAD = 128
S5_T = 8
S5_LANE_GROUPS = LANES // SSM_GROUP
S5_BLOCKS = D_MODEL // LANES
S5_STATE_W = S5_LANE_GROUPS * SSM_STATE
NEG_BIG = -1e30
VMEM_LIMIT = 56 * 1024 * 1024


def _params(n_axes, vmem=None):
    return pltpu.CompilerParams(dimension_semantics=("arbitrary",) * n_axes, vmem_limit_bytes=vmem)


def _rms(x, g):
    return x * lax.rsqrt(jnp.mean(x * x, axis=-1, keepdims=True) + EPS) * g


def _dot(a, b):
    return jnp.dot(a, b, preferred_element_type=F32)


def _norm_matmul_kernel(x_ref, g_ref, w_ref, o_ref):
    h = _rms(x_ref[...], g_ref[...]).astype(BF16)
    o_ref[...] = _dot(h, w_ref[...]).astype(o_ref.dtype)


def norm_matmul(x, g, w, out_dtype, rows=512):
    n, din = x.shape
    dout = w.shape[1]
    return pl.pallas_call(
        _norm_matmul_kernel,
        out_shape=jax.ShapeDtypeStruct((n, dout), out_dtype),
        grid=(n // rows,),
        in_specs=[
            pl.BlockSpec((rows, din), lambda i: (i, 0)),
            pl.BlockSpec((1, din), lambda i: (0, 0)),
            pl.BlockSpec((din, dout), lambda i: (0, 0)),
        ],
        out_specs=pl.BlockSpec((rows, dout), lambda i: (i, 0)),
        compiler_params=_params(1),
        name="norm_matmul",
    )(x, g.reshape(1, din), w)


def _s5_weights(lam_re, lam_im, log_dt, b_re, b_im, c_re, c_im):
    t_steps = S5_T
    lr, li = lam_re.astype(F32), lam_im.astype(F32)
    dt = jnp.exp(log_dt.astype(F32))[:, None]
    mag = jnp.exp(lr * dt)
    ab_re, ab_im = mag * jnp.cos(li * dt), mag * jnp.sin(li * dt)
    den = lr * lr + li * li
    nr, ni = ab_re - 1.0, ab_im
    coef_re = (nr * lr + ni * li) / den
    coef_im = (ni * lr - nr * li) / den
    br, bi = b_re.astype(F32), b_im.astype(F32)
    bb_re = coef_re[..., None] * br - coef_im[..., None] * bi
    bb_im = coef_re[..., None] * bi + coef_im[..., None] * br
    pr, pi = [jnp.ones_like(ab_re)], [jnp.zeros_like(ab_im)]
    for _ in range(t_steps):
        pr.append(pr[-1] * ab_re - pi[-1] * ab_im)
        pi.append(pr[-2] * ab_im + pi[-1] * ab_re)
    pw_re, pw_im = jnp.stack(pr), jnp.stack(pi)
    cr, ci = c_re.astype(F32), c_im.astype(F32)
    eye = jnp.eye(S5_LANE_GROUPS, dtype=F32)
    nb, lg = S5_BLOCKS, S5_LANE_GROUPS

    def blocks(a):
        return a.reshape(a.shape[0], nb, lg, *a.shape[2:])

    dec_re = jnp.stack([pr[t_steps - 1 - s] for s in range(t_steps)])
    dec_im = jnp.stack([pi[t_steps - 1 - s] for s in range(t_steps)])
    si_re = dec_re[..., None] * bb_re[None] - dec_im[..., None] * bb_im[None]
    si_im = dec_re[..., None] * bb_im[None] + dec_im[..., None] * bb_re[None]
    w_in = jnp.stack([jnp.einsum("sjgpm,gh->jsgmhp", blocks(a), eye) for a in (si_re, si_im)], axis=4)
    w_in = w_in.reshape(nb, t_steps * LANES, 2 * S5_STATE_W)
    up_re, up_im = pw_re[1:t_steps + 1], pw_im[1:t_steps + 1]
    so_re = cr[None] * up_re[:, :, None, :] - ci[None] * up_im[:, :, None, :]
    so_im = cr[None] * up_im[:, :, None, :] + ci[None] * up_re[:, :, None, :]
    w_out = jnp.stack([jnp.einsum("tjgnp,gh->jgpthn", blocks(a), eye) for a in (so_re, -so_im)], axis=1)
    w_out = w_out.reshape(nb, 2 * S5_STATE_W, t_steps * LANES)
    lg_re = cr[None] * pw_re[:t_steps, :, None, :] - ci[None] * pw_im[:t_steps, :, None, :]
    lg_im = cr[None] * pw_im[:t_steps, :, None, :] + ci[None] * pw_re[:t_steps, :, None, :]
    k_lag = (jnp.einsum("jgnp,gpm->jgnm", lg_re, bb_re, precision=HIGHEST)
             - jnp.einsum("jgnp,gpm->jgnm", lg_im, bb_im, precision=HIGHEST))
    w_lag = jnp.einsum("jbgnm,gh->bjgmhn", blocks(k_lag), eye).reshape(nb, t_steps, LANES, LANES)
    zero = jnp.zeros((nb, LANES, LANES), F32)
    pairs = []
    for d in range(t_steps // 2):
        rows = []
        for s_local in range(2):
            lag = [2 * d + t_local - s_local for t_local in range(2)]
            rows.append(jnp.concatenate([w_lag[:, j] if j >= 0 else zero for j in lag], axis=2))
        pairs.append(jnp.concatenate(rows, axis=1))
    w_pairs = jnp.stack(pairs, axis=1)
    lam_t = jnp.concatenate([blocks(pw_re[t_steps][None])[0].reshape(nb, 1, S5_STATE_W),
                             blocks(pw_im[t_steps][None])[0].reshape(nb, 1, S5_STATE_W)], axis=2)
    return w_in.astype(BF16), w_out.astype(BF16), w_pairs.astype(BF16), lam_t


def _s5_kernel(*refs, batch, rows_per_batch):
    t_steps = S5_T
    u_refs = refs[:t_steps]
    w_in_ref, w_out_ref, w_pair_ref, lam_ref, d_ref, y_ref, state_ref, carry_ref = refs[t_steps:]
    n_rows = batch * rows_per_batch
    half = S5_STATE_W

    @pl.when(pl.program_id(1) == 0)
    def _():
        carry_ref[...] = jnp.zeros_like(carry_ref)

    us = [r[...].reshape(n_rows, LANES) for r in u_refs]
    u_all = jnp.concatenate(us, axis=1)
    n_slabs = 2 * half // LANES
    contrib = _dot(u_all, w_in_ref[0])
    for k in range(n_slabs):
        state_ref[k] = contrib[:, k * LANES:(k + 1) * LANES]

    lam = lam_ref[0]
    a_re = jnp.broadcast_to(lam[:, :half], (batch, half))
    a_im = jnp.broadcast_to(lam[:, half:], (batch, half))

    def step(c, h):
        rows = pl.ds(c, batch, stride=rows_per_batch)
        x = jnp.concatenate([state_ref[k, rows, :] for k in range(n_slabs)], axis=1)
        for k in range(n_slabs):
            state_ref[k, rows, :] = h[:, k * LANES:(k + 1) * LANES]
        h_re, h_im = h[:, :half], h[:, half:]
        n_re = a_re * h_re - a_im * h_im + x[:, :half]
        n_im = a_re * h_im + a_im * h_re + x[:, half:]
        return jnp.concatenate([n_re, n_im], axis=1)

    carry_ref[...] = lax.fori_loop(0, rows_per_batch, step, carry_ref[...])

    entering = jnp.concatenate([state_ref[k].astype(BF16) for k in range(n_slabs)], axis=1)
    y_state = _dot(entering, w_out_ref[0])
    d_skip = d_ref[...]
    for a in range(t_steps // 2):
        acc = y_state[:, 2 * a * LANES:(2 * a + 2) * LANES]
        for b in range(a + 1):
            acc = acc + _dot(u_all[:, 2 * b * LANES:(2 * b + 2) * LANES], w_pair_ref[0, a - b])
        for t_local in range(2):
            t = 2 * a + t_local
            y = acc[:, t_local * LANES:(t_local + 1) * LANES] + d_skip * us[t].astype(F32)
            y_ref[:, pl.ds(t, rows_per_batch, stride=t_steps), :] = (
                jax.nn.gelu(y).reshape(batch, rows_per_batch, LANES))


def s5_scan(u, w_in, w_out, w_pairs, lam_t, d_skip, batch, seq):
    t_steps = S5_T
    n_chunk_rows = seq // t_steps
    rows_per_batch = min(128, n_chunk_rows)
    u3 = u.reshape(batch, n_chunk_rows, t_steps * D_MODEL)
    blk = (batch, rows_per_batch, LANES)

    def u_spec(s):
        return pl.BlockSpec(blk, lambda j, c, s=s: (0, c, s * S5_BLOCKS + j))

    return pl.pallas_call(
        functools.partial(_s5_kernel, batch=batch, rows_per_batch=rows_per_batch),
        out_shape=jax.ShapeDtypeStruct((batch, seq, D_MODEL), F32),
        grid=(S5_BLOCKS, n_chunk_rows // rows_per_batch),
        in_specs=[u_spec(s) for s in range(t_steps)] + [
            pl.BlockSpec((1, t_steps * LANES, 2 * S5_STATE_W), lambda j, c: (j, 0, 0)),
            pl.BlockSpec((1, 2 * S5_STATE_W, t_steps * LANES), lambda j, c: (j, 0, 0)),
            pl.BlockSpec((1, t_steps // 2, 2 * LANES, 2 * LANES), lambda j, c: (j, 0, 0, 0)),
            pl.BlockSpec((1, 1, 2 * S5_STATE_W), lambda j, c: (j, 0, 0)),
            pl.BlockSpec((1, LANES), lambda j, c: (0, j)),
        ],
        out_specs=pl.BlockSpec((batch, rows_per_batch * t_steps, LANES), lambda j, c: (0, c, j)),
        scratch_shapes=[
            pltpu.VMEM((2 * S5_STATE_W // LANES, batch * rows_per_batch, LANES), F32),
            pltpu.VMEM((batch, 2 * S5_STATE_W), F32),
        ],
        compiler_params=_params(2, VMEM_LIMIT),
        name="s5_scan",
    )(*([u3] * t_steps), w_in, w_out, w_pairs, lam_t, d_skip.reshape(1, D_MODEL))


def _s5_out_kernel(y_ref, x_ref, w_glu_ref, w_out_ref, o_ref):
    y = y_ref[...]
    gate = jax.nn.sigmoid(_dot(y.astype(BF16), w_glu_ref[...]))
    z = (y * gate).astype(BF16)
    o_ref[...] = x_ref[...] + _dot(z, w_out_ref[...])


def s5_out(y, x, w_glu, w_out, rows=512):
    n, d = x.shape
    return pl.pallas_call(
        _s5_out_kernel,
        out_shape=jax.ShapeDtypeStruct((n, d), F32),
        grid=(n // rows,),
        in_specs=[
            pl.BlockSpec((rows, d), lambda i: (i, 0)),
            pl.BlockSpec((rows, d), lambda i: (i, 0)),
            pl.BlockSpec((d, d), lambda i: (0, 0)),
            pl.BlockSpec((d, d), lambda i: (0, 0)),
        ],
        out_specs=pl.BlockSpec((rows, d), lambda i: (i, 0)),
        compiler_params=_params(1),
        name="s5_out",
    )(y, x, w_glu, w_out)


def _swiglu_step(xn_ref, wg_ref, wu_ref, wd_ref, acc_ref):
    xn = xn_ref[...]
    gate = _dot(xn, wg_ref[...].astype(BF16))
    up = _dot(xn, wu_ref[...].astype(BF16))
    mid = (jax.nn.silu(gate) * up).astype(BF16)
    acc_ref[...] += _dot(mid, wd_ref[...].astype(BF16))


def _ffn_kernel(x_ref, g_ref, wg_ref, wu_ref, wd_ref, o_ref, xn_ref, acc_ref):
    f = pl.program_id(1)

    @pl.when(f == 0)
    def _():
        xn_ref[...] = _rms(x_ref[...], g_ref[...]).astype(BF16)
        acc_ref[...] = jnp.zeros_like(acc_ref)

    _swiglu_step(xn_ref, wg_ref, wu_ref, wd_ref, acc_ref)

    @pl.when(f == pl.num_programs(1) - 1)
    def _():
        o_ref[...] = x_ref[...] + acc_ref[...]


def dense_ffn(x, g, wg, wu, wd, rows=1024, ff_tile=896):
    n, d = x.shape
    dff = wg.shape[1]
    return pl.pallas_call(
        _ffn_kernel,
        out_shape=jax.ShapeDtypeStruct((n, d), F32),
        grid=(n // rows, dff // ff_tile),
        in_specs=[
            pl.BlockSpec((rows, d), lambda i, f: (i, 0)),
            pl.BlockSpec((1, d), lambda i, f: (0, 0)),
            pl.BlockSpec((d, ff_tile), lambda i, f: (0, f)),
            pl.BlockSpec((d, ff_tile), lambda i, f: (0, f)),
            pl.BlockSpec((ff_tile, d), lambda i, f: (f, 0)),
        ],
        out_specs=pl.BlockSpec((rows, d), lambda i, f: (i, 0)),
        scratch_shapes=[pltpu.VMEM((rows, d), BF16), pltpu.VMEM((rows, d), F32)],
        compiler_params=_params(2, VMEM_LIMIT),
        name="dense_ffn",
    )(x, g.reshape(1, d), wg, wu, wd)


def _moe_kernel(te_ref, tb_ref, nt_ref, xs_ref, wg_ref, wu_ref, wd_ref, o_ref, xn_ref, acc_ref):
    t, f = pl.program_id(0), pl.program_id(1)

    @pl.when(t < nt_ref[0])
    def _():
        @pl.when(f == 0)
        def _():
            xn_ref[...] = xs_ref[...].astype(BF16)
            acc_ref[...] = jnp.zeros_like(acc_ref)

        _swiglu_step(xn_ref, wg_ref.at[0], wu_ref.at[0], wd_ref.at[0], acc_ref)

        @pl.when(f == pl.num_programs(1) - 1)
        def _():
            o_ref[...] = acc_ref[...]


def moe_experts(xs, tile_expert, tile_block, n_tiles, wg, wu, wd, n_grid_tiles, rows, ff_tile=512):
    m, d = xs.shape
    n_ff = wg.shape[2] // ff_tile

    def ff_idx(t, f, nt):
        return jnp.where(t < nt[0], f, n_ff - 1)

    grid_spec = pltpu.PrefetchScalarGridSpec(
        num_scalar_prefetch=3,
        grid=(n_grid_tiles, n_ff),
        in_specs=[
            pl.BlockSpec((rows, d), lambda t, f, te, tb, nt: (tb[t], 0)),
            pl.BlockSpec((1, d, ff_tile), lambda t, f, te, tb, nt: (te[t], 0, ff_idx(t, f, nt))),
            pl.BlockSpec((1, d, ff_tile), lambda t, f, te, tb, nt: (te[t], 0, ff_idx(t, f, nt))),
            pl.BlockSpec((1, ff_tile, d), lambda t, f, te, tb, nt: (te[t], ff_idx(t, f, nt), 0)),
        ],
        out_specs=pl.BlockSpec((rows, d), lambda t, f, te, tb, nt: (tb[t], 0)),
        scratch_shapes=[pltpu.VMEM((rows, d), BF16), pltpu.VMEM((rows, d), F32)],
    )
    return pl.pallas_call(
        _moe_kernel,
        out_shape=jax.ShapeDtypeStruct((m, d), F32),
        grid_spec=grid_spec,
        compiler_params=_params(2, VMEM_LIMIT),
        name="moe_experts",
    )(tile_expert, tile_block, n_tiles, xs, wg, wu, wd)


def _rope_kernel(pos_ref, freq_ref, cos_ref, sin_ref):
    ang = freq_ref[...] * pos_ref[...]
    cos_ref[...] = jnp.cos(ang)
    sin_ref[...] = jnp.sin(ang)


def rope_tables(positions):
    n = positions.shape[0]
    half = QK_ROPE // 2
    inv_freq = ROPE_BASE ** (-jnp.arange(half, dtype=F32) * (2.0 / QK_ROPE))
    tile = min(n, 4096)
    return pl.pallas_call(
        _rope_kernel,
        out_shape=[jax.ShapeDtypeStruct((half, n), F32)] * 2,
        grid=(n // tile,),
        in_specs=[pl.BlockSpec((1, tile), lambda i: (0, i)), pl.BlockSpec((half, 1), lambda i: (0, 0))],
        out_specs=[pl.BlockSpec((half, tile), lambda i: (0, i))] * 2,
        compiler_params=_params(1),
        name="rope_tables",
    )(positions.astype(F32).reshape(1, n), inv_freq.reshape(half, 1))


def _mla_weights(w_dkv, w_ukv, w_uq):
    half = QK_ROPE // 2
    pad = HEAD_PAD - QK_NOPE - QK_ROPE
    scale = (QK_NOPE + QK_ROPE) ** -0.5
    w_dkv_p = jnp.pad(w_dkv, ((0, 0), (0, LANES - QK_ROPE)))
    ukv = w_ukv.reshape(KV_LORA, N_HEADS, QK_NOPE + V_HEAD)
    k_nope = jnp.pad(ukv[:, :, :QK_NOPE], ((0, 0), (0, 0), (0, HEAD_PAD - QK_NOPE)))
    eye = jnp.eye(half, dtype=F32)
    zer = jnp.zeros((half, half), F32)

    def place(x1_to, x2_to):
        blk = jnp.concatenate([jnp.concatenate([x1_to[0], x1_to[1]], axis=1),
                               jnp.concatenate([x2_to[0], x2_to[1]], axis=1)], axis=0)
        return jnp.pad(blk, ((0, 0), (QK_NOPE, pad)))

    rope_a = place((eye, zer), (zer, eye))
    rope_b = place((zer, eye), (-eye, zer))
    rope_a = jnp.tile(rope_a[:, None, :], (1, N_HEADS, 1))
    rope_b = jnp.tile(rope_b[:, None, :], (1, N_HEADS, 1))
    zrows = jnp.zeros((LANES - QK_ROPE, N_HEADS, HEAD_PAD), F32)
    w_ka = jnp.concatenate([k_nope, rope_a, zrows], axis=0).reshape(KV_LORA + LANES, N_HEADS * HEAD_PAD)
    w_kb = jnp.concatenate([rope_b, zrows], axis=0).reshape(LANES, N_HEADS * HEAD_PAD)
    w_v = ukv[:, :, QK_NOPE:].reshape(KV_LORA, N_HEADS * V_HEAD)
    uq = w_uq.reshape(Q_LORA, N_HEADS, QK_NOPE + QK_ROPE) * scale
    q_x1, q_x2 = uq[:, :, QK_NOPE:QK_NOPE + half], uq[:, :, QK_NOPE + half:]
    w_qa = jnp.pad(uq, ((0, 0), (0, 0), (0, pad))).reshape(Q_LORA, N_HEADS * HEAD_PAD)
    w_qb = jnp.pad(jnp.concatenate([-q_x2, q_x1], axis=2), ((0, 0), (0, 0), (QK_NOPE, pad)))
    w_qb = w_qb.reshape(Q_LORA, N_HEADS * HEAD_PAD)
    return tuple(a.astype(BF16) for a in (w_dkv_p, w_ka, w_kb, w_v, w_qa, w_qb))


def _qkv_kernel(x_ref, gkv_ref, gq_ref, wdkv_ref, glat_ref, wka_ref, wkb_ref, wv_ref, wdq_ref, gql_ref,
                wqa_ref, wqb_ref, cos_ref, sin_ref, q_ref, k_ref, v_ref):
    x = x_ref[...]
    xn = x * lax.rsqrt(jnp.mean(x * x, axis=-1, keepdims=True) + EPS)
    cos = jnp.tile(cos_ref[...], (1, N_HEADS))
    sin = jnp.tile(sin_ref[...], (1, N_HEADS))
    ckr = _dot((xn * gkv_ref[...]).astype(BF16), wdkv_ref[...])
    latent = _rms(ckr[:, :KV_LORA], glat_ref[...]).astype(BF16)
    rope_raw = ckr[:, KV_LORA:].astype(BF16)
    ka = _dot(jnp.concatenate([latent, rope_raw], axis=1), wka_ref[...])
    kb = _dot(rope_raw, wkb_ref[...])
    k_ref[...] = (ka * cos + kb * sin).astype(BF16)
    v_ref[...] = _dot(latent, wv_ref[...]).astype(BF16)
    cq = _rms(_dot((xn * gq_ref[...]).astype(BF16), wdq_ref[...]), gql_ref[...]).astype(BF16)
    qa = _dot(cq, wqa_ref[...])
    qb = _dot(cq, wqb_ref[...])
    q_ref[...] = (qa * cos + qb * sin).astype(BF16)


def qkv_project(x, g_kv, g_q, g_lat, g_qlat, w_dq, mla_w, cos_t, sin_t, rows=256):
    n, d = x.shape
    w_dkv_p, w_ka, w_kb, w_v, w_qa, w_qb = mla_w
    hw = N_HEADS * HEAD_PAD

    def full(a):
        return pl.BlockSpec(a.shape, lambda i: (0,) * a.ndim)

    def row(width):
        return pl.BlockSpec((rows, width), lambda i: (i, 0))

    args = [x, g_kv.reshape(1, d), g_q.reshape(1, d), w_dkv_p, g_lat.reshape(1, KV_LORA), w_ka, w_kb, w_v,
            w_dq, g_qlat.reshape(1, Q_LORA), w_qa, w_qb, cos_t, sin_t]
    in_specs = [row(d)] + [full(a) for a in args[1:12]] + [row(HEAD_PAD), row(HEAD_PAD)]
    return pl.pallas_call(
        _qkv_kernel,
        out_shape=[jax.ShapeDtypeStruct((n, hw), BF16), jax.ShapeDtypeStruct((n, hw), BF16),
                   jax.ShapeDtypeStruct((n, N_HEADS * V_HEAD), BF16)],
        grid=(n // rows,),
        in_specs=in_specs,
        out_specs=[row(hw), row(hw), row(N_HEADS * V_HEAD)],
        compiler_params=_params(1, VMEM_LIMIT),
        name="qkv_project",
    )(*args)


def _attn_kernel(q_ref, k_ref, v_ref, o_ref, m_ref, l_ref, acc_ref, *, tile):
    qi = pl.program_id(2)
    row_chunk = lax.broadcasted_iota(I32, (tile, tile), 0) // CHUNK
    col_chunk = lax.broadcasted_iota(I32, (tile, tile), 1) // CHUNK
    diag_mask = col_chunk <= row_chunk
    outs = []
    for h in range(2):
        qh = q_ref[:, h * HEAD_PAD:(h + 1) * HEAD_PAD]
        m_ref[...] = jnp.full_like(m_ref, NEG_BIG)
        l_ref[...] = jnp.zeros_like(l_ref)
        acc_ref[...] = jnp.zeros_like(acc_ref)

        def kv_tile(j, masked, h=h, qh=qh):
            rows = pl.ds(pl.multiple_of(j * tile, tile), tile)
            kt = k_ref[rows, h * HEAD_PAD:(h + 1) * HEAD_PAD]
            s = lax.dot_general(qh, kt, (((1,), (1,)), ((), ())), preferred_element_type=F32)
            if masked:
                s = jnp.where(diag_mask, s, NEG_BIG)
            m_old = m_ref[...]
            m_new = jnp.maximum(m_old, jnp.max(s, axis=1, keepdims=True))
            alpha = jnp.exp(m_old - m_new)
            p = jnp.exp(s - m_new)
            l_ref[...] = alpha * l_ref[...] + jnp.sum(p, axis=1, keepdims=True)
            acc_ref[...] = alpha * acc_ref[...] + _dot(p.astype(BF16), v_ref[rows, :])
            m_ref[...] = m_new

        def full_tile(j, carry):
            kv_tile(j, False)
            return carry

        lax.fori_loop(0, qi, full_tile, 0)
        kv_tile(qi, True)
        outs.append(acc_ref[...] / l_ref[...])
    lane = lax.broadcasted_iota(I32, outs[0].shape, 1)
    o_ref[...] = jnp.where(lane < V_HEAD, outs[0], outs[1]).astype(BF16)


def attention(q, k, v, batch, seq, tile=512):
    tile = min(tile, seq)
    n_q = seq // tile
    pair_qk = 2 * HEAD_PAD
    pair_v = 2 * V_HEAD
    return pl.pallas_call(
        functools.partial(_attn_kernel, tile=tile),
        out_shape=jax.ShapeDtypeStruct(v.shape, BF16),
        grid=(batch, N_HEADS // 2, n_q),
        in_specs=[
            pl.BlockSpec((tile, pair_qk), lambda b, hp, i: (b * n_q + i, hp)),
            pl.BlockSpec((seq, pair_qk), lambda b, hp, i: (b, hp)),
            pl.BlockSpec((seq, pair_v), lambda b, hp, i: (b, hp)),
        ],
        out_specs=pl.BlockSpec((tile, pair_v), lambda b, hp, i: (b * n_q + i, hp)),
        scratch_shapes=[pltpu.VMEM((tile, 1), F32), pltpu.VMEM((tile, 1), F32), pltpu.VMEM((tile, pair_v), F32)],
        compiler_params=_params(3, VMEM_LIMIT),
        name="attention",
    )(q, k, v)


def _residual_matmul_kernel(a_ref, x_ref, w_ref, o_ref):
    o_ref[...] = x_ref[...] + _dot(a_ref[...], w_ref[...])


def residual_matmul(a, x, w, rows=512):
    n, d = x.shape
    return pl.pallas_call(
        _residual_matmul_kernel,
        out_shape=jax.ShapeDtypeStruct((n, d), F32),
        grid=(n // rows,),
        in_specs=[
            pl.BlockSpec((rows, a.shape[1]), lambda i: (i, 0)),
            pl.BlockSpec((rows, d), lambda i: (i, 0)),
            pl.BlockSpec(w.shape, lambda i: (0, 0)),
        ],
        out_specs=pl.BlockSpec((rows, d), lambda i: (i, 0)),
        compiler_params=_params(1),
        name="residual_matmul",
    )(a, x, w)


ROUTE_COLS = 8


def _router_kernel(x_ref, g_ref, rw_ref, tri_ref, h_ref, route_ref, cnt_ref, carry_ref):
    @pl.when(pl.program_id(0) == 0)
    def _():
        carry_ref[...] = jnp.zeros_like(carry_ref)

    h = _rms(x_ref[...], g_ref[...])
    h_ref[...] = h
    logits = jnp.dot(h, rw_ref[...], precision=HIGHEST, preferred_element_type=F32)
    col = lax.broadcasted_iota(I32, logits.shape, 1)
    m1 = jnp.max(logits, axis=1, keepdims=True)
    i1 = jnp.min(jnp.where(logits == m1, col, N_EXPERTS), axis=1, keepdims=True)
    rest = jnp.where(col == i1, -jnp.inf, logits)
    m2 = jnp.max(rest, axis=1, keepdims=True)
    i2 = jnp.min(jnp.where(rest == m2, col, N_EXPERTS), axis=1, keepdims=True)
    e2 = jnp.exp(m2 - m1)
    w1 = 1.0 / (1.0 + e2)
    w2 = e2 / (1.0 + e2)
    sel = jnp.where((col == i1) | (col == i2), 1.0, 0.0)
    incl = _dot(tri_ref[...], sel.astype(BF16))
    carry = carry_ref[0:1, :]
    excl = incl - sel + carry
    r1 = jnp.sum(jnp.where(col == i1, excl, 0.0), axis=1, keepdims=True)
    r2 = jnp.sum(jnp.where(col == i2, excl, 0.0), axis=1, keepdims=True)
    total = carry + incl[incl.shape[0] - 1:, :]
    carry_ref[...] = jnp.broadcast_to(total, carry_ref.shape)
    cnt_ref[...] = jnp.broadcast_to(total, cnt_ref.shape)
    route = jnp.where(col == 0, i1.astype(F32), 0.0)
    for c, val in ((1, i2.astype(F32)), (2, r1), (3, r2), (4, w1), (5, w2)):
        route = jnp.where(col == c, val, route)
    route_ref[...] = route


def router(x, g, router_w, rows=1024):
    n, d = x.shape
    rows = min(rows, n)
    tri = (jnp.arange(rows)[:, None] >= jnp.arange(rows)[None, :]).astype(BF16)
    return pl.pallas_call(
        _router_kernel,
        out_shape=[jax.ShapeDtypeStruct((n, d), F32), jax.ShapeDtypeStruct((n, ROUTE_COLS), F32),
                   jax.ShapeDtypeStruct((8, N_EXPERTS), F32)],
        grid=(n // rows,),
        in_specs=[
            pl.BlockSpec((rows, d), lambda i: (i, 0)),
            pl.BlockSpec((1, d), lambda i: (0, 0)),
            pl.BlockSpec((d, N_EXPERTS), lambda i: (0, 0)),
            pl.BlockSpec((rows, rows), lambda i: (0, 0)),
        ],
        out_specs=[
            pl.BlockSpec((rows, d), lambda i: (i, 0)),
            pl.BlockSpec((rows, ROUTE_COLS), lambda i: (i, 0)),
            pl.BlockSpec((8, N_EXPERTS), lambda i: (0, 0)),
        ],
        scratch_shapes=[pltpu.VMEM((8, N_EXPERTS), F32)],
        compiler_params=_params(1, VMEM_LIMIT),
        name="router",
    )(x, g.reshape(1, d), router_w, tri)


def _row_copy(src_ref, src_row, dst_ref, dst_row, sem):
    return pltpu.make_async_copy(src_ref.at[pl.ds(src_row, 1)], dst_ref.at[pl.ds(dst_row, 1)], sem)


def _dispatch_kernel(zoff_ref, p0_ref, p1_ref, h_ref, xs_ref, zero_ref, sem, zsem, *, rows, pad_rows):
    @pl.when(pl.program_id(0) == 0)
    def _():
        zero_ref[...] = jnp.zeros_like(zero_ref)
        copies = [pltpu.make_async_copy(
            zero_ref, xs_ref.at[pl.ds(pl.multiple_of(zoff_ref[e], SUBLANES), pad_rows)], zsem)
            for e in range(N_EXPERTS)]
        for c in copies:
            c.start()
        for c in copies:
            c.wait()

    def issue(r, carry):
        _row_copy(h_ref, r, xs_ref, p0_ref[r], sem).start()
        _row_copy(h_ref, r, xs_ref, p1_ref[r], sem).start()
        return carry

    lax.fori_loop(0, rows, issue, 0)

    def drain(r, carry):
        _row_copy(h_ref, 0, xs_ref, 0, sem).wait()
        _row_copy(h_ref, 0, xs_ref, 0, sem).wait()
        return carry

    lax.fori_loop(0, rows, drain, 0)


def dispatch(h, pos0, pos1, zero_off, total_rows, pad_rows, rows=1024):
    n, d = h.shape
    rows = min(rows, n)
    grid_spec = pltpu.PrefetchScalarGridSpec(
        num_scalar_prefetch=1,
        grid=(n // rows,),
        in_specs=[
            pl.BlockSpec((rows,), lambda i, z: (i,), memory_space=pltpu.SMEM),
            pl.BlockSpec((rows,), lambda i, z: (i,), memory_space=pltpu.SMEM),
            pl.BlockSpec((rows, d), lambda i, z: (i, 0)),
        ],
        out_specs=pl.BlockSpec(memory_space=pl.ANY),
        scratch_shapes=[pltpu.VMEM((pad_rows, d), F32), pltpu.SemaphoreType.DMA, pltpu.SemaphoreType.DMA],
    )
    return pl.pallas_call(
        functools.partial(_dispatch_kernel, rows=rows, pad_rows=pad_rows),
        out_shape=jax.ShapeDtypeStruct((total_rows, d), F32),
        grid_spec=grid_spec,
        compiler_params=_params(1, VMEM_LIMIT),
        name="dispatch",
    )(zero_off, pos0, pos1, h)


def _combine_kernel(p0_ref, p1_ref, ys_ref, x_ref, w_ref, g_ref, o_ref, buf0, buf1, sem, *, rows):
    def issue(r, carry):
        _row_copy(ys_ref, p0_ref[r], buf0, r, sem).start()
        _row_copy(ys_ref, p1_ref[r], buf1, r, sem).start()
        return carry

    lax.fori_loop(0, rows, issue, 0)

    def drain(r, carry):
        _row_copy(ys_ref, 0, buf0, 0, sem).wait()
        _row_copy(ys_ref, 0, buf1, 0, sem).wait()
        return carry

    lax.fori_loop(0, rows, drain, 0)
    w = w_ref[...]
    y = x_ref[...] + w[:, 4:5] * buf0[...] + w[:, 5:6] * buf1[...]
    o_ref[...] = _rms(y, g_ref[...])


def combine(ys, pos0, pos1, x, route, g, rows=512):
    n, d = x.shape
    rows = min(rows, n)
    return pl.pallas_call(
        functools.partial(_combine_kernel, rows=rows),
        out_shape=jax.ShapeDtypeStruct((n, d), F32),
        grid=(n // rows,),
        in_specs=[
            pl.BlockSpec((rows,), lambda i: (i,), memory_space=pltpu.SMEM),
            pl.BlockSpec((rows,), lambda i: (i,), memory_space=pltpu.SMEM),
            pl.BlockSpec(memory_space=pl.ANY),
            pl.BlockSpec((rows, d), lambda i: (i, 0)),
            pl.BlockSpec((rows, ROUTE_COLS), lambda i: (i, 0)),
            pl.BlockSpec((1, d), lambda i: (0, 0)),
        ],
        out_specs=pl.BlockSpec((rows, d), lambda i: (i, 0)),
        scratch_shapes=[pltpu.VMEM((rows, d), F32), pltpu.VMEM((rows, d), F32), pltpu.SemaphoreType.DMA],
        compiler_params=_params(1, VMEM_LIMIT),
        name="combine",
    )(pos0, pos1, ys, x, route, g.reshape(1, d))


def moe_layer(x, g_ffn, g_final, router_w, wg, wu, wd, tile_rows=1024):
    n, d = x.shape
    tile_rows = min(tile_rows, n)
    h, route, counts = router(x, g_ffn, router_w)
    counts = counts[0].astype(I32)
    tiles = (counts + tile_rows - 1) // tile_rows
    tile_start = jnp.cumsum(tiles) - tiles
    row_start = tile_start * tile_rows
    n_tiles = jnp.sum(tiles)
    max_tiles = (2 * n) // tile_rows + N_EXPERTS
    total_rows = (max_tiles + 2) * tile_rows
    e0, e1 = route[:, 0].astype(I32), route[:, 1].astype(I32)
    pos0 = row_start[e0] + route[:, 2].astype(I32)
    pos1 = row_start[e1] + route[:, 3].astype(I32)
    t_idx = jnp.arange(max_tiles, dtype=I32)
    tile_expert = jnp.sum(t_idx[:, None] >= (tile_start + tiles)[None, :], axis=1).astype(I32)
    last = n_tiles - 1
    tile_expert = jnp.where(t_idx < n_tiles, jnp.minimum(tile_expert, N_EXPERTS - 1), tile_expert[last])
    tile_block = jnp.minimum(t_idx, last).astype(I32)
    zero_off = ((row_start + counts) // SUBLANES * SUBLANES).astype(I32)
    xs = dispatch(h, pos0, pos1, zero_off, total_rows, tile_rows + SUBLANES)
    ys = moe_experts(xs, tile_expert, tile_block, n_tiles.reshape(1).astype(I32), wg, wu, wd, max_tiles, tile_rows)
    return combine(ys, pos0, pos1, x, route, g_final)


def kernel(x, positions, norm_mix, norm_ffn, final_norm, s5_w_in, s5_lambda_re, s5_lambda_im, s5_log_dt, s5_b_re, s5_b_im, s5_c_re, s5_c_im, s5_d, s5_w_glu, s5_w_out, kv_norm, w_dkv, kv_latent_norm, w_ukv, w_dq, q_latent_norm, w_uq, w_o, ffn_w_gate, ffn_w_up, ffn_w_down, router_w, moe_w_gate, moe_w_up, moe_w_down):
    batch, seq, d = x.shape
    n = batch * seq
    x0 = x.reshape(n, d)
    u = norm_matmul(x0, norm_mix[0], s5_w_in[0].astype(BF16), BF16)
    s5_w = _s5_weights(s5_lambda_re[0], s5_lambda_im[0], s5_log_dt[0], s5_b_re[0], s5_b_im[0],
                       s5_c_re[0], s5_c_im[0])
    y = s5_scan(u, *s5_w, s5_d[0], batch, seq).reshape(n, d)
    x1 = s5_out(y, x0, s5_w_glu[0].astype(BF16), s5_w_out[0].astype(BF16))
    x2 = dense_ffn(x1, norm_ffn[0], ffn_w_gate[0].astype(BF16), ffn_w_up[0].astype(BF16),
                   ffn_w_down[0].astype(BF16))
    cos_t, sin_t = rope_tables(positions.reshape(n))
    half = QK_ROPE // 2
    pad = HEAD_PAD - QK_NOPE - QK_ROPE
    cos_n, sin_n = cos_t.T, sin_t.T
    cos_tab = jnp.concatenate([jnp.ones((n, QK_NOPE), F32), cos_n, cos_n, jnp.zeros((n, pad), F32)], axis=1)
    sin_tab = jnp.concatenate([jnp.zeros((n, QK_NOPE), F32), sin_n, sin_n, jnp.zeros((n, pad), F32)], axis=1)
    mla_w = _mla_weights(w_dkv, w_ukv, w_uq[0])
    q, k, v = qkv_project(x2, kv_norm, norm_mix[1], kv_latent_norm, q_latent_norm[0], w_dq[0].astype(BF16),
                          mla_w, cos_tab, sin_tab)
    o = attention(q, k, v, batch, seq)
    x3 = residual_matmul(o, x2, w_o[0].astype(BF16))
    out = moe_layer(x3, norm_ffn[1], final_norm, router_w[0], moe_w_gate[0], moe_w_up[0], moe_w_down[0])
    return out.reshape(batch, seq, d)
```

```python
import functools

import jax
import jax.numpy as jnp
from jax import lax
from jax.experimental import pallas as pl
from jax.experimental.pallas import tpu as pltpu

F32 = jnp.float32
BF16 = jnp.bfloat16
I32 = jnp.int32
HIGHEST = lax.Precision.HIGHEST

D_MODEL = 1024
CHUNK = 64
SSM_GROUP = 16
SSM_GROUPS = D_MODEL // SSM_GROUP
SSM_STATE = 64
N_HEADS = 16
QK_NOPE = 64
QK_ROPE = 32
V_HEAD = 64
Q_LORA = 512
KV_LORA = 256
ROPE_BASE = 10000.0
D_FF = 2688
N_EXPERTS = 8
MOE_FF = 3584
EPS = 1e-6

LANES = 128
SUBLANES = 8
HEAD_PAD = 128
S5_T = 8
S5_LANE_GROUPS = LANES // SSM_GROUP
S5_BLOCKS = D_MODEL // LANES
S5_STATE_W = S5_LANE_GROUPS * SSM_STATE
NEG_BIG = -1e30
LOG2_E = 1.4426950408889634
ONE_LANE_EVEN = V_HEAD
ONE_LANE_ODD = 0
VMEM_LIMIT = 56 * 1024 * 1024


def _params(n_axes, vmem=None):
    return pltpu.CompilerParams(dimension_semantics=("arbitrary",) * n_axes, vmem_limit_bytes=vmem)


def _rms(x, g):
    return x * lax.rsqrt(jnp.mean(x * x, axis=-1, keepdims=True) + EPS) * g


def _dot(a, b):
    return jnp.dot(a, b, preferred_element_type=F32)


def _norm_matmul_chunked_kernel(x_ref, g_ref, w_ref, o_ref, slab_ref, *, t_steps):
    h = _rms(x_ref[...], g_ref[...]).astype(BF16)
    res = _dot(h, w_ref[...])
    rows, dout = res.shape
    for k in range(dout // LANES):
        slab_ref[k] = res[:, k * LANES:(k + 1) * LANES]
    for s in range(t_steps):
        for k in range(dout // LANES):
            lanes = slice(s * dout + k * LANES, s * dout + (k + 1) * LANES)
            o_ref[:, lanes] = slab_ref[k, pl.ds(s, rows // t_steps, stride=t_steps), :].astype(o_ref.dtype)


def norm_matmul_chunked(x, g, w, out_dtype, t_steps, rows=512):
    n, din = x.shape
    dout = w.shape[1]
    return pl.pallas_call(
        functools.partial(_norm_matmul_chunked_kernel, t_steps=t_steps),
        out_shape=jax.ShapeDtypeStruct((n // t_steps, t_steps * dout), out_dtype),
        grid=(n // rows,),
        in_specs=[
            pl.BlockSpec((rows, din), lambda i: (i, 0)),
            pl.BlockSpec((1, din), lambda i: (0, 0)),
            pl.BlockSpec((din, dout), lambda i: (0, 0)),
        ],
        out_specs=pl.BlockSpec((rows // t_steps, t_steps * dout), lambda i: (i, 0)),
        scratch_shapes=[pltpu.VMEM((dout // LANES, rows, LANES), F32)],
        compiler_params=_params(1),
        name="norm_matmul_chunked",
    )(x, g.reshape(1, din), w)


def _s5_weights(lam_re, lam_im, log_dt, b_re, b_im, c_re, c_im):
    t_steps = S5_T
    lr, li = lam_re.astype(F32), lam_im.astype(F32)
    dt = jnp.exp(log_dt.astype(F32))[:, None]
    mag = jnp.exp(lr * dt)
    ab_re, ab_im = mag * jnp.cos(li * dt), mag * jnp.sin(li * dt)
    den = lr * lr + li * li
    nr, ni = ab_re - 1.0, ab_im
    coef_re = (nr * lr + ni * li) / den
    coef_im = (ni * lr - nr * li) / den
    br, bi = b_re.astype(F32), b_im.astype(F32)
    bb_re = coef_re[..., None] * br - coef_im[..., None] * bi
    bb_im = coef_re[..., None] * bi + coef_im[..., None] * br
    pr, pi = [jnp.ones_like(ab_re)], [jnp.zeros_like(ab_im)]
    for _ in range(t_steps):
        pr.append(pr[-1] * ab_re - pi[-1] * ab_im)
        pi.append(pr[-2] * ab_im + pi[-1] * ab_re)
    pw_re, pw_im = jnp.stack(pr), jnp.stack(pi)
    cr, ci = c_re.astype(F32), c_im.astype(F32)
    eye = jnp.eye(S5_LANE_GROUPS, dtype=BF16)
    nb, lg = S5_BLOCKS, S5_LANE_GROUPS

    def blocks(a):
        return a.reshape(a.shape[0], nb, lg, *a.shape[2:])

    def block_diag(a, perm, eye_axes):
        a = jnp.transpose(blocks(a), perm).astype(BF16)[..., None, :]
        shape = [1] * a.ndim
        shape[eye_axes[0]], shape[eye_axes[1]] = lg, lg
        return a * eye.reshape(shape)

    dec_re = jnp.stack([pr[t_steps - 1 - s] for s in range(t_steps)])
    dec_im = jnp.stack([pi[t_steps - 1 - s] for s in range(t_steps)])
    si_re = dec_re[..., None] * bb_re[None] - dec_im[..., None] * bb_im[None]
    si_im = dec_re[..., None] * bb_im[None] + dec_im[..., None] * bb_re[None]
    w_in = jnp.stack([block_diag(a, (1, 0, 2, 4, 3), (2, 4)) for a in (si_re, si_im)], axis=4)
    w_in = w_in.reshape(nb, t_steps * LANES, 2 * S5_STATE_W)
    up_re, up_im = pw_re[1:t_steps + 1], pw_im[1:t_steps + 1]
    so_re = cr[None] * up_re[:, :, None, :] - ci[None] * up_im[:, :, None, :]
    so_im = cr[None] * up_im[:, :, None, :] + ci[None] * up_re[:, :, None, :]
    w_out = jnp.stack([block_diag(a, (1, 2, 4, 0, 3), (1, 4)) for a in (so_re, -so_im)], axis=1)
    w_out = w_out.reshape(nb, 2 * S5_STATE_W, t_steps * LANES)
    lg_re = cr[None] * pw_re[:t_steps, :, None, :] - ci[None] * pw_im[:t_steps, :, None, :]
    lg_im = cr[None] * pw_im[:t_steps, :, None, :] + ci[None] * pw_re[:t_steps, :, None, :]
    k_lag = (jnp.einsum("jgnp,gpm->jgnm", lg_re, bb_re, precision=HIGHEST)
             - jnp.einsum("jgnp,gpm->jgnm", lg_im, bb_im, precision=HIGHEST))
    w_lag = block_diag(k_lag, (1, 0, 2, 4, 3), (2, 4)).reshape(nb, t_steps, LANES, LANES)
    zero = jnp.zeros((nb, LANES, LANES), BF16)
    pairs = []
    for d in range(t_steps // 2):
        rows = []
        for s_local in range(2):
            lag = [2 * d + t_local - s_local for t_local in range(2)]
            rows.append(jnp.concatenate([w_lag[:, j] if j >= 0 else zero for j in lag], axis=2))
        pairs.append(jnp.concatenate(rows, axis=1))
    w_pairs = jnp.stack(pairs, axis=1)
    lam_t = jnp.concatenate([blocks(pw_re[t_steps][None])[0].reshape(nb, 1, S5_STATE_W),
                             blocks(pw_im[t_steps][None])[0].reshape(nb, 1, S5_STATE_W)], axis=2)
    return w_in, w_out, w_pairs, lam_t


def _s5_kernel(*refs, batch, rows_per_batch):
    t_steps = S5_T
    u_refs = refs[:t_steps]
    w_in_ref, w_out_ref, w_pair_ref, lam_ref, d_ref, y_ref, state_ref, carry_ref = refs[t_steps:]
    n_rows = batch * rows_per_batch
    half = S5_STATE_W

    @pl.when(pl.program_id(1) == 0)
    def _():
        carry_ref[...] = jnp.zeros_like(carry_ref)

    us = [r[...].reshape(n_rows, LANES) for r in u_refs]
    u_all = jnp.concatenate(us, axis=1)
    n_slabs = 2 * half // LANES
    contrib = _dot(u_all, w_in_ref[0])
    pitch = rows_per_batch + SUBLANES
    for k in range(n_slabs):
        for b in range(batch):
            state_ref[k, b * pitch:b * pitch + rows_per_batch, :] = (
                contrib[b * rows_per_batch:(b + 1) * rows_per_batch, k * LANES:(k + 1) * LANES])

    lam = lam_ref[0]
    a_re = jnp.broadcast_to(lam[:, :half], (batch, half))
    a_im = jnp.broadcast_to(lam[:, half:], (batch, half))

    def step(c, h):
        rows = pl.ds(c, batch, stride=pitch)
        x = jnp.concatenate([state_ref[k, rows, :] for k in range(n_slabs)], axis=1)
        for k in range(n_slabs):
            state_ref[k, rows, :] = h[:, k * LANES:(k + 1) * LANES]
        h_re, h_im = h[:, :half], h[:, half:]
        n_re = a_re * h_re - a_im * h_im + x[:, :half]
        n_im = a_re * h_im + a_im * h_re + x[:, half:]
        return jnp.concatenate([n_re, n_im], axis=1)

    carry_ref[...] = lax.fori_loop(0, rows_per_batch, step, carry_ref[...])

    entering = jnp.concatenate(
        [jnp.concatenate([state_ref[k, b * pitch:b * pitch + rows_per_batch, :] for b in range(batch)], axis=0)
         for k in range(n_slabs)], axis=1).astype(BF16)
    y_state = _dot(entering, w_out_ref[0])
    d_skip = d_ref[...]
    for a in range(t_steps // 2):
        acc = y_state[:, 2 * a * LANES:(2 * a + 2) * LANES]
        for b in range(a + 1):
            acc = acc + _dot(u_all[:, 2 * b * LANES:(2 * b + 2) * LANES], w_pair_ref[0, a - b])
        for t_local in range(2):
            t = 2 * a + t_local
            y = acc[:, t_local * LANES:(t_local + 1) * LANES] + d_skip * us[t].astype(F32)
            y_ref[:, pl.ds(t, rows_per_batch, stride=t_steps), :] = (
                jax.nn.gelu(y).reshape(batch, rows_per_batch, LANES))


def s5_scan(u, w_in, w_out, w_pairs, lam_t, d_skip, batch, seq):
    t_steps = S5_T
    n_chunk_rows = seq // t_steps
    rows_per_batch = min(128, n_chunk_rows)
    u3 = u.reshape(batch, n_chunk_rows, t_steps * D_MODEL)
    blk = (batch, rows_per_batch, LANES)

    def u_spec(s):
        return pl.BlockSpec(blk, lambda j, c, s=s: (0, c, s * S5_BLOCKS + j))

    return pl.pallas_call(
        functools.partial(_s5_kernel, batch=batch, rows_per_batch=rows_per_batch),
        out_shape=jax.ShapeDtypeStruct((batch, seq, D_MODEL), F32),
        grid=(S5_BLOCKS, n_chunk_rows // rows_per_batch),
        in_specs=[u_spec(s) for s in range(t_steps)] + [
            pl.BlockSpec((1, t_steps * LANES, 2 * S5_STATE_W), lambda j, c: (j, 0, 0)),
            pl.BlockSpec((1, 2 * S5_STATE_W, t_steps * LANES), lambda j, c: (j, 0, 0)),
            pl.BlockSpec((1, t_steps // 2, 2 * LANES, 2 * LANES), lambda j, c: (j, 0, 0, 0)),
            pl.BlockSpec((1, 1, 2 * S5_STATE_W), lambda j, c: (j, 0, 0)),
            pl.BlockSpec((1, LANES), lambda j, c: (0, j)),
        ],
        out_specs=pl.BlockSpec((batch, rows_per_batch * t_steps, LANES), lambda j, c: (0, c, j)),
        scratch_shapes=[
            pltpu.VMEM((2 * S5_STATE_W // LANES, batch * (rows_per_batch + SUBLANES), LANES), F32),
            pltpu.VMEM((batch, 2 * S5_STATE_W), F32),
        ],
        compiler_params=_params(2, VMEM_LIMIT),
        name="s5_scan",
    )(*([u3] * t_steps), w_in, w_out, w_pairs, lam_t, d_skip.reshape(1, D_MODEL))


def _s5_out_kernel(y_ref, x_ref, w_glu_ref, w_out_ref, o_ref):
    y = y_ref[...]
    gate = jax.nn.sigmoid(_dot(y.astype(BF16), w_glu_ref[...]))
    z = (y * gate).astype(BF16)
    o_ref[...] = x_ref[...] + _dot(z, w_out_ref[...])


def s5_out(y, x, w_glu, w_out, rows=512):
    n, d = x.shape
    return pl.pallas_call(
        _s5_out_kernel,
        out_shape=jax.ShapeDtypeStruct((n, d), F32),
        grid=(n // rows,),
        in_specs=[
            pl.BlockSpec((rows, d), lambda i: (i, 0)),
            pl.BlockSpec((rows, d), lambda i: (i, 0)),
            pl.BlockSpec((d, d), lambda i: (0, 0)),
            pl.BlockSpec((d, d), lambda i: (0, 0)),
        ],
        out_specs=pl.BlockSpec((rows, d), lambda i: (i, 0)),
        compiler_params=_params(1),
        name="s5_out",
    )(y, x, w_glu, w_out)


def _ffn_kernel(x_ref, g_ref, wg_ref, wu_ref, wd_ref, o_ref):
    x = x_ref[...]
    xn = _rms(x, g_ref[...]).astype(BF16)
    mid = (jax.nn.silu(_dot(xn, wg_ref[...])) * _dot(xn, wu_ref[...])).astype(BF16)
    o_ref[...] = x + _dot(mid, wd_ref[...])


def dense_ffn(x, g, wg, wu, wd, rows=512):
    n, d = x.shape

    def resident(a):
        return pl.BlockSpec(a.shape, lambda i: (0, 0), pipeline_mode=pl.Buffered(1))

    return pl.pallas_call(
        _ffn_kernel,
        out_shape=jax.ShapeDtypeStruct((n, d), F32),
        grid=(n // rows,),
        in_specs=[
            pl.BlockSpec((rows, d), lambda i: (i, 0)),
            pl.BlockSpec((1, d), lambda i: (0, 0)),
            resident(wg), resident(wu), resident(wd),
        ],
        out_specs=pl.BlockSpec((rows, d), lambda i: (i, 0)),
        compiler_params=_params(1, VMEM_LIMIT),
        name="dense_ffn",
    )(x, g.reshape(1, d), wg, wu, wd)


def _moe_kernel(te_ref, nt_ref, xs_ref, wg_ref, wu_ref, wd_ref, o_ref):
    t = pl.program_id(0)

    @pl.when(t < nt_ref[0])
    def _():
        xn = xs_ref[...].astype(BF16)
        mid = (jax.nn.silu(_dot(xn, wg_ref[0])) * _dot(xn, wu_ref[0])).astype(BF16)
        o_ref[...] = _dot(mid, wd_ref[0])

    @pl.when(t >= nt_ref[0])
    def _():
        o_ref[...] = jnp.zeros_like(o_ref)


def moe_experts(xs, tile_expert, n_tiles, wg, wu, wd, n_grid_tiles, rows):
    d = xs.shape[1]

    def resident(w):
        return pl.BlockSpec((1,) + w.shape[1:], lambda t, te, nt: (te[t], 0, 0), pipeline_mode=pl.Buffered(1))

    grid_spec = pltpu.PrefetchScalarGridSpec(
        num_scalar_prefetch=2,
        grid=(n_grid_tiles,),
        in_specs=[pl.BlockSpec((rows, d), lambda t, te, nt: (t, 0)), resident(wg), resident(wu), resident(wd)],
        out_specs=pl.BlockSpec((rows, d), lambda t, te, nt: (t, 0)),
    )
    return pl.pallas_call(
        _moe_kernel,
        out_shape=jax.ShapeDtypeStruct((n_grid_tiles * rows, d), F32),
        grid_spec=grid_spec,
        compiler_params=_params(1, VMEM_LIMIT),
        name="moe_experts",
    )(tile_expert, n_tiles, xs, wg, wu, wd)


def _rope_kernel(pos_ref, freq_ref, cos_ref, sin_ref):
    ang = freq_ref[...] * pos_ref[...]
    cos, sin = jnp.cos(ang), jnp.sin(ang)
    n = ang.shape[1]
    pad = HEAD_PAD - QK_NOPE - QK_ROPE
    cos_ref[...] = jnp.concatenate([jnp.ones((QK_NOPE, n), F32), cos, cos, jnp.zeros((pad, n), F32)], axis=0)
    sin_ref[...] = jnp.concatenate([jnp.zeros((QK_NOPE, n), F32), sin, sin, jnp.zeros((pad, n), F32)], axis=0)


def rope_tables(positions):
    n = positions.shape[0]
    half = QK_ROPE // 2
    inv_freq = ROPE_BASE ** (-jnp.arange(half, dtype=F32) * (2.0 / QK_ROPE))
    tile = min(n, 4096)
    return pl.pallas_call(
        _rope_kernel,
        out_shape=[jax.ShapeDtypeStruct((HEAD_PAD, n), F32)] * 2,
        grid=(n // tile,),
        in_specs=[pl.BlockSpec((1, tile), lambda i: (0, i)), pl.BlockSpec((half, 1), lambda i: (0, 0))],
        out_specs=[pl.BlockSpec((HEAD_PAD, tile), lambda i: (0, i))] * 2,
        compiler_params=_params(1),
        name="rope_tables",
    )(positions.astype(F32).reshape(1, n), inv_freq.reshape(half, 1))


def _mla_weights(w_dkv, w_ukv, w_uq):
    half = QK_ROPE // 2
    pad = HEAD_PAD - QK_NOPE - QK_ROPE
    scale = (QK_NOPE + QK_ROPE) ** -0.5
    w_dkv_p = jnp.pad(w_dkv, ((0, 0), (0, LANES - QK_ROPE)))
    ukv = w_ukv.reshape(KV_LORA, N_HEADS, QK_NOPE + V_HEAD)
    k_nope = jnp.pad(ukv[:, :, :QK_NOPE], ((0, 0), (0, 0), (0, HEAD_PAD - QK_NOPE)))
    eye = jnp.eye(half, dtype=F32)
    zer = jnp.zeros((half, half), F32)

    def place(x1_to, x2_to):
        blk = jnp.concatenate([jnp.concatenate([x1_to[0], x1_to[1]], axis=1),
                               jnp.concatenate([x2_to[0], x2_to[1]], axis=1)], axis=0)
        return jnp.pad(blk, ((0, 0), (QK_NOPE, pad)))

    rope_a = place((eye, zer), (zer, eye))
    rope_b = place((zer, eye), (-eye, zer))
    rope_a = jnp.tile(rope_a[:, None, :], (1, N_HEADS, 1))
    rope_b = jnp.tile(rope_b[:, None, :], (1, N_HEADS, 1))
    zrows = jnp.zeros((LANES - QK_ROPE, N_HEADS, HEAD_PAD), F32)
    w_ka = jnp.concatenate([k_nope, rope_a, zrows], axis=0).reshape(KV_LORA + LANES, N_HEADS * HEAD_PAD)
    w_kb = jnp.concatenate([rope_b, zrows], axis=0).reshape(LANES, N_HEADS * HEAD_PAD)
    vh = ukv[:, :, QK_NOPE:].reshape(KV_LORA, N_HEADS // 2, 2, V_HEAD)
    w_v = jnp.concatenate([jnp.pad(vh[:, :, 0], ((0, 0), (0, 0), (0, HEAD_PAD - V_HEAD))),
                           jnp.pad(vh[:, :, 1], ((0, 0), (0, 0), (HEAD_PAD - V_HEAD, 0)))], axis=2)
    w_v = w_v.reshape(KV_LORA, N_HEADS * HEAD_PAD)
    uq = w_uq.reshape(Q_LORA, N_HEADS, QK_NOPE + QK_ROPE) * (scale * LOG2_E)
    q_x1, q_x2 = uq[:, :, QK_NOPE:QK_NOPE + half], uq[:, :, QK_NOPE + half:]
    w_qa = jnp.pad(uq, ((0, 0), (0, 0), (0, pad))).reshape(Q_LORA, N_HEADS * HEAD_PAD)
    w_qb = jnp.pad(jnp.concatenate([-q_x2, q_x1], axis=2), ((0, 0), (0, 0), (QK_NOPE, pad)))
    w_qb = w_qb.reshape(Q_LORA, N_HEADS * HEAD_PAD)
    return tuple(a.astype(BF16) for a in (w_dkv_p, w_ka, w_kb, w_v, w_qa, w_qb))


def _qkv_kernel(x_ref, gkv_ref, gq_ref, wdkv_ref, glat_ref, wka_ref, wkb_ref, wv_ref, vone_ref, wdq_ref, gql_ref,
                wqa_ref, wqb_ref, cos_ref, sin_ref, q_ref, k_ref, v_ref):
    x = x_ref[...]
    xn = x * lax.rsqrt(jnp.mean(x * x, axis=-1, keepdims=True) + EPS)
    cos = jnp.tile(cos_ref[...].T, (1, N_HEADS))
    sin = jnp.tile(sin_ref[...].T, (1, N_HEADS))
    ckr = _dot((xn * gkv_ref[...]).astype(BF16), wdkv_ref[...])
    latent = _rms(ckr[:, :KV_LORA], glat_ref[...]).astype(BF16)
    rope_raw = ckr[:, KV_LORA:].astype(BF16)
    ka = _dot(jnp.concatenate([latent, rope_raw], axis=1), wka_ref[...])
    kb = _dot(rope_raw, wkb_ref[...])
    k_ref[...] = (ka * cos + kb * sin).astype(BF16)
    v_ref[...] = (_dot(latent, wv_ref[...]) + vone_ref[...]).astype(BF16)
    cq = _rms(_dot((xn * gq_ref[...]).astype(BF16), wdq_ref[...]), gql_ref[...]).astype(BF16)
    qa = _dot(cq, wqa_ref[...])
    qb = _dot(cq, wqb_ref[...])
    q_ref[...] = (qa * cos + qb * sin).astype(BF16)


def qkv_project(x, g_kv, g_q, g_lat, g_qlat, w_dq, mla_w, cos_t, sin_t, rows=256):
    n, d = x.shape
    w_dkv_p, w_ka, w_kb, w_v, w_qa, w_qb = mla_w
    hw = N_HEADS * HEAD_PAD
    lane = jnp.arange(hw, dtype=I32) % (2 * HEAD_PAD)
    v_one = ((lane == ONE_LANE_EVEN) | (lane == HEAD_PAD + ONE_LANE_ODD)).astype(F32).reshape(1, hw)

    def full(a):
        return pl.BlockSpec(a.shape, lambda i: (0,) * a.ndim)

    def row(width):
        return pl.BlockSpec((rows, width), lambda i: (i, 0))

    args = [x, g_kv.reshape(1, d), g_q.reshape(1, d), w_dkv_p, g_lat.reshape(1, KV_LORA), w_ka, w_kb, w_v, v_one,
            w_dq, g_qlat.reshape(1, Q_LORA), w_qa, w_qb, cos_t, sin_t]
    table = pl.BlockSpec((HEAD_PAD, rows), lambda i: (0, i))
    in_specs = [row(d)] + [full(a) for a in args[1:13]] + [table, table]
    return pl.pallas_call(
        _qkv_kernel,
        out_shape=[jax.ShapeDtypeStruct((n, hw), BF16)] * 3,
        grid=(n // rows,),
        in_specs=in_specs,
        out_specs=[row(hw)] * 3,
        compiler_params=_params(1, VMEM_LIMIT),
        name="qkv_project",
    )(*args)


def _attn_kernel(q_ref, k_ref, v_ref, o_ref, m_ref, acc_ref, *, tq, tk):
    qi = pl.program_id(2)
    row_chunk = lax.broadcasted_iota(I32, (tq, tk), 0) // CHUNK
    col_chunk = lax.broadcasted_iota(I32, (tq, tk), 1) // CHUNK
    m_ref[...] = jnp.full_like(m_ref, NEG_BIG)
    acc_ref[...] = jnp.zeros_like(acc_ref)
    qs = [q_ref[:, h * HEAD_PAD:(h + 1) * HEAD_PAD] for h in range(2)]

    def kv_tile(j, mask):
        rows = pl.ds(pl.multiple_of(j * tk, tk), tk)
        for h in range(2):
            cols = slice(h * HEAD_PAD, (h + 1) * HEAD_PAD)
            s = lax.dot_general(qs[h], k_ref[rows, cols], (((1,), (1,)), ((), ())), preferred_element_type=F32)
            if mask is not None:
                s = jnp.where(mask, s, NEG_BIG)
            m_old = m_ref[h]
            m_new = jnp.maximum(m_old, jnp.max(s, axis=1, keepdims=True))
            p = jnp.exp2(s - jnp.tile(m_new, (1, tk // LANES))).astype(BF16)
            acc_ref[h] = jnp.exp2(m_old - m_new) * acc_ref[h] + _dot(p, v_ref[rows, cols])
            m_ref[h] = m_new

    def full_tile(j, carry):
        kv_tile(j, None)
        return carry

    n_full = qi * (tq // tk)
    lax.fori_loop(0, n_full, full_tile, 0)
    for d in range(tq // tk):
        kv_tile(n_full + d, col_chunk + (d * tk) // CHUNK <= row_chunk)
    even, odd = acc_ref[0], acc_ref[1]
    even = even / even[:, ONE_LANE_EVEN:ONE_LANE_EVEN + 1]
    odd = odd / odd[:, ONE_LANE_ODD:ONE_LANE_ODD + 1]
    lane = lax.broadcasted_iota(I32, even.shape, 1)
    o_ref[...] = jnp.where(lane < V_HEAD, even, odd).astype(BF16)


def attention(q, k, v, batch, seq, tq=1024, tk=1024):
    tq, tk = min(tq, seq), min(tk, seq)
    n_q = seq // tq
    pair = 2 * HEAD_PAD
    return pl.pallas_call(
        functools.partial(_attn_kernel, tq=tq, tk=tk),
        out_shape=jax.ShapeDtypeStruct((batch * seq, N_HEADS * V_HEAD), BF16),
        grid=(batch, N_HEADS // 2, n_q),
        in_specs=[
            pl.BlockSpec((tq, pair), lambda b, hp, i: (b * n_q + i, hp)),
            pl.BlockSpec((seq, pair), lambda b, hp, i: (b, hp)),
            pl.BlockSpec((seq, pair), lambda b, hp, i: (b, hp)),
        ],
        out_specs=pl.BlockSpec((tq, 2 * V_HEAD), lambda b, hp, i: (b * n_q + i, hp)),
        scratch_shapes=[pltpu.VMEM((2, tq, LANES), F32), pltpu.VMEM((2, tq, HEAD_PAD), F32)],
        compiler_params=_params(3, VMEM_LIMIT),
        name="attention",
    )(q, k, v)


def _residual_matmul_kernel(a_ref, x_ref, w_ref, o_ref):
    o_ref[...] = x_ref[...] + _dot(a_ref[...], w_ref[...])


def residual_matmul(a, x, w, rows=512):
    n, d = x.shape
    return pl.pallas_call(
        _residual_matmul_kernel,
        out_shape=jax.ShapeDtypeStruct((n, d), F32),
        grid=(n // rows,),
        in_specs=[
            pl.BlockSpec((rows, a.shape[1]), lambda i: (i, 0)),
            pl.BlockSpec((rows, d), lambda i: (i, 0)),
            pl.BlockSpec(w.shape, lambda i: (0, 0)),
        ],
        out_specs=pl.BlockSpec((rows, d), lambda i: (i, 0)),
        compiler_params=_params(1),
        name="residual_matmul",
    )(a, x, w)


ROUTE_COLS = 8


def _router_kernel(x_ref, g_ref, rw_ref, tri_ref, h_ref, route_ref, cnt_ref, carry_ref):
    @pl.when(pl.program_id(0) == 0)
    def _():
        carry_ref[...] = jnp.zeros_like(carry_ref)

    h = _rms(x_ref[...], g_ref[...])
    h_ref[...] = h
    h_hi = h.astype(BF16)
    h_lo = (h - h_hi.astype(F32)).astype(BF16)
    by_hi = _dot(h_hi, rw_ref[...])
    by_lo = _dot(h_lo, rw_ref[...])
    logits = by_hi[:, :N_EXPERTS] + by_hi[:, N_EXPERTS:] + by_lo[:, :N_EXPERTS]
    col = lax.broadcasted_iota(I32, logits.shape, 1)
    m1 = jnp.max(logits, axis=1, keepdims=True)
    i1 = jnp.min(jnp.where(logits == m1, col, N_EXPERTS), axis=1, keepdims=True)
    rest = jnp.where(col == i1, -jnp.inf, logits)
    m2 = jnp.max(rest, axis=1, keepdims=True)
    i2 = jnp.min(jnp.where(rest == m2, col, N_EXPERTS), axis=1, keepdims=True)
    e2 = jnp.exp(m2 - m1)
    w1 = 1.0 / (1.0 + e2)
    w2 = e2 / (1.0 + e2)
    sel = jnp.where((col == i1) | (col == i2), 1.0, 0.0)
    incl = _dot(tri_ref[...], sel.astype(BF16))
    carry = carry_ref[0:1, :]
    excl = incl - sel + carry
    r1 = jnp.sum(jnp.where(col == i1, excl, 0.0), axis=1, keepdims=True)
    r2 = jnp.sum(jnp.where(col == i2, excl, 0.0), axis=1, keepdims=True)
    total = carry + incl[incl.shape[0] - 1:, :]
    carry_ref[...] = jnp.broadcast_to(total, carry_ref.shape)
    cnt_ref[...] = jnp.broadcast_to(total, cnt_ref.shape)
    route = jnp.where(col == 0, i1.astype(F32), 0.0)
    for c, val in ((1, i2.astype(F32)), (2, r1), (3, r2), (4, w1), (5, w2)):
        route = jnp.where(col == c, val, route)
    route_ref[...] = route


def router(x, g, router_w, rows=1024):
    n, d = x.shape
    rows = min(rows, n)
    tri = (jnp.arange(rows)[:, None] >= jnp.arange(rows)[None, :]).astype(BF16)
    rw_hi = router_w.astype(BF16)
    rw_lo = (router_w - rw_hi.astype(F32)).astype(BF16)
    router_w = jnp.concatenate([rw_hi, rw_lo], axis=1)
    return pl.pallas_call(
        _router_kernel,
        out_shape=[jax.ShapeDtypeStruct((n, d), F32), jax.ShapeDtypeStruct((n, ROUTE_COLS), F32),
                   jax.ShapeDtypeStruct((8, N_EXPERTS), F32)],
        grid=(n // rows,),
        in_specs=[
            pl.BlockSpec((rows, d), lambda i: (i, 0)),
            pl.BlockSpec((1, d), lambda i: (0, 0)),
            pl.BlockSpec((d, 2 * N_EXPERTS), lambda i: (0, 0)),
            pl.BlockSpec((rows, rows), lambda i: (0, 0)),
        ],
        out_specs=[
            pl.BlockSpec((rows, d), lambda i: (i, 0)),
            pl.BlockSpec((rows, ROUTE_COLS), lambda i: (i, 0)),
            pl.BlockSpec((8, N_EXPERTS), lambda i: (0, 0)),
        ],
        scratch_shapes=[pltpu.VMEM((8, N_EXPERTS), F32)],
        compiler_params=_params(1, VMEM_LIMIT),
        name="router",
    )(x, g.reshape(1, d), router_w, tri)


def _row_copy(src_ref, src_row, dst_ref, dst_row, sem):
    return pltpu.make_async_copy(src_ref.at[pl.ds(src_row, 1)], dst_ref.at[pl.ds(dst_row, 1)], sem)


ISSUE_UNROLL = 8


def _dispatch_kernel(zoff_ref, nt_ref, p0_ref, p1_ref, h_ref, xs_ref, zero_ref, sem, zsem, *,
                     rows, tile_rows, n_buf_tiles):
    @pl.when(pl.program_id(0) == 0)
    def _():
        zero_ref[...] = jnp.zeros_like(zero_ref)
        copies = [pltpu.make_async_copy(
            zero_ref, xs_ref.at[pl.ds(pl.multiple_of(zoff_ref[e], SUBLANES), tile_rows + SUBLANES)], zsem)
            for e in range(N_EXPERTS)]
        for c in copies:
            c.start()
        for c in copies:
            c.wait()
        def fill_idle(t, carry):
            c = pltpu.make_async_copy(zero_ref.at[pl.ds(0, tile_rows)],
                                      xs_ref.at[pl.ds(pl.multiple_of(t * tile_rows, tile_rows), tile_rows)], zsem)
            c.start()
            c.wait()
            return carry

        lax.fori_loop(nt_ref[0], n_buf_tiles, fill_idle, 0)

    def issue(r, carry):
        _row_copy(h_ref, r, xs_ref, p0_ref[r], sem).start()
        _row_copy(h_ref, r, xs_ref, p1_ref[r], sem).start()
        return carry

    lax.fori_loop(0, rows, issue, 0, unroll=ISSUE_UNROLL)
    for _ in range(2):
        pltpu.make_async_copy(h_ref, xs_ref.at[pl.ds(0, rows)], sem).wait()


def dispatch(h, pos0, pos1, zero_off, n_tiles, n_buf_tiles, tile_rows, rows=1024):
    n, d = h.shape
    rows = min(rows, n)
    grid_spec = pltpu.PrefetchScalarGridSpec(
        num_scalar_prefetch=2,
        grid=(n // rows,),
        in_specs=[
            pl.BlockSpec((rows,), lambda i, z, t: (i,), memory_space=pltpu.SMEM),
            pl.BlockSpec((rows,), lambda i, z, t: (i,), memory_space=pltpu.SMEM),
            pl.BlockSpec((rows, d), lambda i, z, t: (i, 0)),
        ],
        out_specs=pl.BlockSpec(memory_space=pl.ANY),
        scratch_shapes=[pltpu.VMEM((tile_rows + SUBLANES, d), F32), pltpu.SemaphoreType.DMA,
                        pltpu.SemaphoreType.DMA],
    )
    return pl.pallas_call(
        functools.partial(_dispatch_kernel, rows=rows, tile_rows=tile_rows, n_buf_tiles=n_buf_tiles),
        out_shape=jax.ShapeDtypeStruct((n_buf_tiles * tile_rows, d), F32),
        grid_spec=grid_spec,
        compiler_params=_params(1, VMEM_LIMIT),
        name="dispatch",
    )(zero_off, n_tiles, pos0, pos1, h)


def _combine_kernel(p0_ref, p1_ref, q0_ref, q1_ref, ys_ref, x_ref, w_ref, g_ref, o_ref, buf, sem, *, rows):
    i = pl.program_id(0)
    slot = i % 2

    def issue(a_ref, b_ref, to_slot):
        def body(r, carry):
            _row_copy(ys_ref, a_ref[r], buf.at[to_slot, 0], r, sem.at[to_slot]).start()
            _row_copy(ys_ref, b_ref[r], buf.at[to_slot, 1], r, sem.at[to_slot]).start()
            return carry

        lax.fori_loop(0, rows, body, 0, unroll=ISSUE_UNROLL)

    @pl.when(i == 0)
    def _():
        issue(p0_ref, p1_ref, 0)

    @pl.when(i + 1 < pl.num_programs(0))
    def _():
        issue(q0_ref, q1_ref, 1 - slot)

    for k in range(2):
        pltpu.make_async_copy(ys_ref.at[pl.ds(0, rows)], buf.at[slot, k], sem.at[slot]).wait()
    w = w_ref[...]
    y = x_ref[...] + w[:, 4:5] * buf[slot, 0] + w[:, 5:6] * buf[slot, 1]
    o_ref[...] = _rms(y, g_ref[...])


def combine(ys, pos0, pos1, x, route, g, rows=512):
    n, d = x.shape
    rows = min(rows, n)
    n_blocks = n // rows

    def cur(i):
        return (i,)

    def nxt(i):
        return (jnp.minimum(i + 1, n_blocks - 1),)

    return pl.pallas_call(
        functools.partial(_combine_kernel, rows=rows),
        out_shape=jax.ShapeDtypeStruct((n, d), F32),
        grid=(n_blocks,),
        in_specs=[
            pl.BlockSpec((rows,), cur, memory_space=pltpu.SMEM),
            pl.BlockSpec((rows,), cur, memory_space=pltpu.SMEM),
            pl.BlockSpec((rows,), nxt, memory_space=pltpu.SMEM),
            pl.BlockSpec((rows,), nxt, memory_space=pltpu.SMEM),
            pl.BlockSpec(memory_space=pl.ANY),
            pl.BlockSpec((rows, d), lambda i: (i, 0)),
            pl.BlockSpec((rows, ROUTE_COLS), lambda i: (i, 0)),
            pl.BlockSpec((1, d), lambda i: (0, 0)),
        ],
        out_specs=pl.BlockSpec((rows, d), lambda i: (i, 0)),
        scratch_shapes=[pltpu.VMEM((2, 2, rows, d), F32), pltpu.SemaphoreType.DMA((2,))],
        compiler_params=_params(1, VMEM_LIMIT),
        name="combine",
    )(pos0, pos1, pos0, pos1, ys, x, route, g.reshape(1, d))


def moe_layer(x, g_ffn, g_final, router_w, wg, wu, wd, tile_rows=512):
    n, d = x.shape
    tile_rows = min(tile_rows, n)
    h, route, counts = router(x, g_ffn, router_w)
    counts = counts[0].astype(I32)
    tiles = (counts + tile_rows - 1) // tile_rows
    tile_start = jnp.cumsum(tiles) - tiles
    row_start = tile_start * tile_rows
    n_tiles = jnp.sum(tiles).reshape(1).astype(I32)
    max_tiles = (2 * n) // tile_rows + N_EXPERTS
    n_buf_tiles = max_tiles + 2
    e0, e1 = route[:, 0].astype(I32), route[:, 1].astype(I32)
    pos0 = row_start[e0] + route[:, 2].astype(I32)
    pos1 = row_start[e1] + route[:, 3].astype(I32)
    t_idx = jnp.arange(max_tiles, dtype=I32)
    tile_expert = jnp.sum(t_idx[:, None] >= (tile_start + tiles)[None, :], axis=1).astype(I32)
    tile_expert = jnp.where(t_idx < n_tiles[0], jnp.minimum(tile_expert, N_EXPERTS - 1),
                            tile_expert[n_tiles[0] - 1])
    zero_off = ((row_start + counts) // SUBLANES * SUBLANES).astype(I32)
    xs = dispatch(h, pos0, pos1, zero_off, n_tiles, n_buf_tiles, tile_rows)
    ys = moe_experts(xs, tile_expert, n_tiles, wg, wu, wd, max_tiles, tile_rows)
    return combine(ys, pos0, pos1, x, route, g_final)


def kernel(x, positions, norm_mix, norm_ffn, final_norm, s5_w_in, s5_lambda_re, s5_lambda_im, s5_log_dt, s5_b_re, s5_b_im, s5_c_re, s5_c_im, s5_d, s5_w_glu, s5_w_out, kv_norm, w_dkv, kv_latent_norm, w_ukv, w_dq, q_latent_norm, w_uq, w_o, ffn_w_gate, ffn_w_up, ffn_w_down, router_w, moe_w_gate, moe_w_up, moe_w_down):
    batch, seq, d = x.shape
    n = batch * seq
    x0 = x.reshape(n, d)
    u = norm_matmul_chunked(x0, norm_mix[0], s5_w_in[0].astype(BF16), BF16, S5_T)
    s5_w = _s5_weights(s5_lambda_re[0], s5_lambda_im[0], s5_log_dt[0], s5_b_re[0], s5_b_im[0],
                       s5_c_re[0], s5_c_im[0])
    y = s5_scan(u, *s5_w, s5_d[0], batch, seq).reshape(n, d)
    x1 = s5_out(y, x0, s5_w_glu[0].astype(BF16), s5_w_out[0].astype(BF16))
    x2 = dense_ffn(x1, norm_ffn[0], ffn_w_gate[0].astype(BF16), ffn_w_up[0].astype(BF16),
                   ffn_w_down[0].astype(BF16))
    cos_tab, sin_tab = rope_tables(positions.reshape(n))
    mla_w = _mla_weights(w_dkv, w_ukv, w_uq[0])
    q, k, v = qkv_project(x2, kv_norm, norm_mix[1], kv_latent_norm, q_latent_norm[0], w_dq[0].astype(BF16),
                          mla_w, cos_tab, sin_tab)
    o = attention(q, k, v, batch, seq)
    x3 = residual_matmul(o, x2, w_o[0].astype(BF16))
    out = moe_layer(x3, norm_ffn[1], final_norm, router_w[0], moe_w_gate[0].astype(BF16),
                    moe_w_up[0].astype(BF16), moe_w_down[0].astype(BF16))
    return out.reshape(batch, seq, d)
```

```python
import functools

import jax
import jax.numpy as jnp
from jax import lax
from jax.experimental import pallas as pl
from jax.experimental.pallas import tpu as pltpu

F32 = jnp.float32
BF16 = jnp.bfloat16
I32 = jnp.int32
HIGHEST = lax.Precision.HIGHEST

D_MODEL = 1024
CHUNK = 64
SSM_GROUP = 16
SSM_GROUPS = D_MODEL // SSM_GROUP
SSM_STATE = 64
N_HEADS = 16
QK_NOPE = 64
QK_ROPE = 32
V_HEAD = 64
Q_LORA = 512
KV_LORA = 256
ROPE_BASE = 10000.0
D_FF = 2688
N_EXPERTS = 8
MOE_FF = 3584
EPS = 1e-6

LANES = 128
SUBLANES = 8
HEAD_PAD = 128
S5_T = 8
S5_LANE_GROUPS = LANES // SSM_GROUP
S5_BLOCKS = D_MODEL // LANES
S5_STATE_W = S5_LANE_GROUPS * SSM_STATE
NEG_BIG = -1e30
LOG2_E = 1.4426950408889634
ONE_LANE_EVEN = V_HEAD
ONE_LANE_ODD = 0
VMEM_LIMIT = 56 * 1024 * 1024


def _params(n_axes, vmem=None):
    return pltpu.CompilerParams(dimension_semantics=("arbitrary",) * n_axes, vmem_limit_bytes=vmem)


def _rms(x, g):
    return x * lax.rsqrt(jnp.mean(x * x, axis=-1, keepdims=True) + EPS) * g


def _dot(a, b):
    return jnp.dot(a, b, preferred_element_type=F32)


def _norm_matmul_chunked_kernel(x_ref, g_ref, w_ref, o_ref, slab_ref, *, t_steps):
    h = _rms(x_ref[...], g_ref[...]).astype(BF16)
    res = _dot(h, w_ref[...])
    rows, dout = res.shape
    for k in range(dout // LANES):
        slab_ref[k] = res[:, k * LANES:(k + 1) * LANES]
    for s in range(t_steps):
        for k in range(dout // LANES):
            lanes = slice(s * dout + k * LANES, s * dout + (k + 1) * LANES)
            o_ref[:, lanes] = slab_ref[k, pl.ds(s, rows // t_steps, stride=t_steps), :].astype(o_ref.dtype)


def norm_matmul_chunked(x, g, w, out_dtype, t_steps, rows=512):
    n, din = x.shape
    dout = w.shape[1]
    return pl.pallas_call(
        functools.partial(_norm_matmul_chunked_kernel, t_steps=t_steps),
        out_shape=jax.ShapeDtypeStruct((n // t_steps, t_steps * dout), out_dtype),
        grid=(n // rows,),
        in_specs=[
            pl.BlockSpec((rows, din), lambda i: (i, 0)),
            pl.BlockSpec((1, din), lambda i: (0, 0)),
            pl.BlockSpec((din, dout), lambda i: (0, 0)),
        ],
        out_specs=pl.BlockSpec((rows // t_steps, t_steps * dout), lambda i: (i, 0)),
        scratch_shapes=[pltpu.VMEM((dout // LANES, rows, LANES), F32)],
        compiler_params=_params(1),
        name="norm_matmul_chunked",
    )(x, g.reshape(1, din), w)


def _s5_weights(lam_re, lam_im, log_dt, b_re, b_im, c_re, c_im):
    t_steps = S5_T
    lr, li = lam_re.astype(F32), lam_im.astype(F32)
    dt = jnp.exp(log_dt.astype(F32))[:, None]
    mag = jnp.exp(lr * dt)
    ab_re, ab_im = mag * jnp.cos(li * dt), mag * jnp.sin(li * dt)
    den = lr * lr + li * li
    nr, ni = ab_re - 1.0, ab_im
    coef_re = (nr * lr + ni * li) / den
    coef_im = (ni * lr - nr * li) / den
    br, bi = b_re.astype(F32), b_im.astype(F32)
    bb_re = coef_re[..., None] * br - coef_im[..., None] * bi
    bb_im = coef_re[..., None] * bi + coef_im[..., None] * br
    pr, pi = [jnp.ones_like(ab_re)], [jnp.zeros_like(ab_im)]
    for _ in range(t_steps):
        pr.append(pr[-1] * ab_re - pi[-1] * ab_im)
        pi.append(pr[-2] * ab_im + pi[-1] * ab_re)
    pw_re, pw_im = jnp.stack(pr), jnp.stack(pi)
    cr, ci = c_re.astype(F32), c_im.astype(F32)
    eye = jnp.eye(S5_LANE_GROUPS, dtype=BF16)
    nb, lg = S5_BLOCKS, S5_LANE_GROUPS

    def blocks(a):
        return a.reshape(a.shape[0], nb, lg, *a.shape[2:])

    def block_diag(a, perm, eye_axes):
        a = jnp.transpose(blocks(a), perm).astype(BF16)[..., None, :]
        shape = [1] * a.ndim
        shape[eye_axes[0]], shape[eye_axes[1]] = lg, lg
        return a * eye.reshape(shape)

    dec_re = jnp.stack([pr[t_steps - 1 - s] for s in range(t_steps)])
    dec_im = jnp.stack([pi[t_steps - 1 - s] for s in range(t_steps)])
    si_re = dec_re[..., None] * bb_re[None] - dec_im[..., None] * bb_im[None]
    si_im = dec_re[..., None] * bb_im[None] + dec_im[..., None] * bb_re[None]
    w_in = jnp.stack([block_diag(a, (1, 0, 2, 4, 3), (2, 4)) for a in (si_re, si_im)], axis=4)
    w_in = w_in.reshape(nb, t_steps * LANES, 2 * S5_STATE_W)
    up_re, up_im = pw_re[1:t_steps + 1], pw_im[1:t_steps + 1]
    so_re = cr[None] * up_re[:, :, None, :] - ci[None] * up_im[:, :, None, :]
    so_im = cr[None] * up_im[:, :, None, :] + ci[None] * up_re[:, :, None, :]
    w_out = jnp.stack([block_diag(a, (1, 2, 4, 0, 3), (1, 4)) for a in (so_re, -so_im)], axis=1)
    w_out = w_out.reshape(nb, 2 * S5_STATE_W, t_steps * LANES)
    lg_re = cr[None] * pw_re[:t_steps, :, None, :] - ci[None] * pw_im[:t_steps, :, None, :]
    lg_im = cr[None] * pw_im[:t_steps, :, None, :] + ci[None] * pw_re[:t_steps, :, None, :]
    k_lag = (jnp.einsum("jgnp,gpm->jgnm", lg_re, bb_re, precision=HIGHEST)
             - jnp.einsum("jgnp,gpm->jgnm", lg_im, bb_im, precision=HIGHEST))
    w_lag = block_diag(k_lag, (1, 0, 2, 4, 3), (2, 4)).reshape(nb, t_steps, LANES, LANES)
    zero = jnp.zeros((nb, LANES, LANES), BF16)
    pairs = []
    for d in range(t_steps // 2):
        rows = []
        for s_local in range(2):
            lag = [2 * d + t_local - s_local for t_local in range(2)]
            rows.append(jnp.concatenate([w_lag[:, j] if j >= 0 else zero for j in lag], axis=2))
        pairs.append(jnp.concatenate(rows, axis=1))
    w_pairs = jnp.stack(pairs, axis=1)
    lam_t = jnp.concatenate([blocks(pw_re[t_steps][None])[0].reshape(nb, 1, S5_STATE_W),
                             blocks(pw_im[t_steps][None])[0].reshape(nb, 1, S5_STATE_W)], axis=2)
    return w_in, w_out, w_pairs, lam_t


def _s5_kernel(*refs, batch, rows_per_batch):
    t_steps = S5_T
    u_refs = refs[:t_steps]
    w_in_ref, w_out_ref, w_pair_ref, lam_ref, d_ref, y_ref, state_ref, carry_ref = refs[t_steps:]
    n_rows = batch * rows_per_batch
    half = S5_STATE_W

    @pl.when(pl.program_id(1) == 0)
    def _():
        carry_ref[...] = jnp.zeros_like(carry_ref)

    us = [r[...].reshape(n_rows, LANES) for r in u_refs]
    u_all = jnp.concatenate(us, axis=1)
    n_slabs = 2 * half // LANES
    contrib = _dot(u_all, w_in_ref[0])
    pitch = rows_per_batch + SUBLANES
    for k in range(n_slabs):
        for b in range(batch):
            state_ref[k, b * pitch:b * pitch + rows_per_batch, :] = (
                contrib[b * rows_per_batch:(b + 1) * rows_per_batch, k * LANES:(k + 1) * LANES])

    lam = lam_ref[0]
    a_re = jnp.broadcast_to(lam[:, :half], (batch, half))
    a_im = jnp.broadcast_to(lam[:, half:], (batch, half))

    def step(c, h):
        rows = pl.ds(c, batch, stride=pitch)
        x = jnp.concatenate([state_ref[k, rows, :] for k in range(n_slabs)], axis=1)
        for k in range(n_slabs):
            state_ref[k, rows, :] = h[:, k * LANES:(k + 1) * LANES]
        h_re, h_im = h[:, :half], h[:, half:]
        n_re = a_re * h_re - a_im * h_im + x[:, :half]
        n_im = a_re * h_im + a_im * h_re + x[:, half:]
        return jnp.concatenate([n_re, n_im], axis=1)

    carry_ref[...] = lax.fori_loop(0, rows_per_batch, step, carry_ref[...])

    entering = jnp.concatenate(
        [jnp.concatenate([state_ref[k, b * pitch:b * pitch + rows_per_batch, :] for b in range(batch)], axis=0)
         for k in range(n_slabs)], axis=1).astype(BF16)
    y_state = _dot(entering, w_out_ref[0])
    d_skip = d_ref[...]
    for a in range(t_steps // 2):
        acc = y_state[:, 2 * a * LANES:(2 * a + 2) * LANES]
        for b in range(a + 1):
            acc = acc + _dot(u_all[:, 2 * b * LANES:(2 * b + 2) * LANES], w_pair_ref[0, a - b])
        for t_local in range(2):
            t = 2 * a + t_local
            y = acc[:, t_local * LANES:(t_local + 1) * LANES] + d_skip * us[t].astype(F32)
            y_ref[:, pl.ds(t, rows_per_batch, stride=t_steps), :] = (
                jax.nn.gelu(y).reshape(batch, rows_per_batch, LANES))


def s5_scan(u, w_in, w_out, w_pairs, lam_t, d_skip, batch, seq):
    t_steps = S5_T
    n_chunk_rows = seq // t_steps
    rows_per_batch = min(128, n_chunk_rows)
    u3 = u.reshape(batch, n_chunk_rows, t_steps * D_MODEL)
    blk = (batch, rows_per_batch, LANES)

    def u_spec(s):
        return pl.BlockSpec(blk, lambda j, c, s=s: (0, c, s * S5_BLOCKS + j))

    return pl.pallas_call(
        functools.partial(_s5_kernel, batch=batch, rows_per_batch=rows_per_batch),
        out_shape=jax.ShapeDtypeStruct((batch, seq, D_MODEL), F32),
        grid=(S5_BLOCKS, n_chunk_rows // rows_per_batch),
        in_specs=[u_spec(s) for s in range(t_steps)] + [
            pl.BlockSpec((1, t_steps * LANES, 2 * S5_STATE_W), lambda j, c: (j, 0, 0)),
            pl.BlockSpec((1, 2 * S5_STATE_W, t_steps * LANES), lambda j, c: (j, 0, 0)),
            pl.BlockSpec((1, t_steps // 2, 2 * LANES, 2 * LANES), lambda j, c: (j, 0, 0, 0)),
            pl.BlockSpec((1, 1, 2 * S5_STATE_W), lambda j, c: (j, 0, 0)),
            pl.BlockSpec((1, LANES), lambda j, c: (0, j)),
        ],
        out_specs=pl.BlockSpec((batch, rows_per_batch * t_steps, LANES), lambda j, c: (0, c, j)),
        scratch_shapes=[
            pltpu.VMEM((2 * S5_STATE_W // LANES, batch * (rows_per_batch + SUBLANES), LANES), F32),
            pltpu.VMEM((batch, 2 * S5_STATE_W), F32),
        ],
        compiler_params=_params(2, VMEM_LIMIT),
        name="s5_scan",
    )(*([u3] * t_steps), w_in, w_out, w_pairs, lam_t, d_skip.reshape(1, D_MODEL))


def _s5_out_kernel(y_ref, x_ref, w_glu_ref, w_out_ref, o_ref):
    y = y_ref[...]
    gate = jax.nn.sigmoid(_dot(y.astype(BF16), w_glu_ref[...]))
    z = (y * gate).astype(BF16)
    o_ref[...] = x_ref[...] + _dot(z, w_out_ref[...])


def s5_out(y, x, w_glu, w_out, rows=512):
    n, d = x.shape
    return pl.pallas_call(
        _s5_out_kernel,
        out_shape=jax.ShapeDtypeStruct((n, d), F32),
        grid=(n // rows,),
        in_specs=[
            pl.BlockSpec((rows, d), lambda i: (i, 0)),
            pl.BlockSpec((rows, d), lambda i: (i, 0)),
            pl.BlockSpec((d, d), lambda i: (0, 0)),
            pl.BlockSpec((d, d), lambda i: (0, 0)),
        ],
        out_specs=pl.BlockSpec((rows, d), lambda i: (i, 0)),
        compiler_params=_params(1),
        name="s5_out",
    )(y, x, w_glu, w_out)


def _ffn_kernel(x_ref, g_ref, wg_ref, wu_ref, wd_ref, o_ref):
    x = x_ref[...]
    xn = _rms(x, g_ref[...]).astype(BF16)
    mid = (jax.nn.silu(_dot(xn, wg_ref[...])) * _dot(xn, wu_ref[...])).astype(BF16)
    o_ref[...] = x + _dot(mid, wd_ref[...])


def dense_ffn(x, g, wg, wu, wd, rows=512):
    n, d = x.shape

    def resident(a):
        return pl.BlockSpec(a.shape, lambda i: (0, 0), pipeline_mode=pl.Buffered(1))

    return pl.pallas_call(
        _ffn_kernel,
        out_shape=jax.ShapeDtypeStruct((n, d), F32),
        grid=(n // rows,),
        in_specs=[
            pl.BlockSpec((rows, d), lambda i: (i, 0)),
            pl.BlockSpec((1, d), lambda i: (0, 0)),
            resident(wg), resident(wu), resident(wd),
        ],
        out_specs=pl.BlockSpec((rows, d), lambda i: (i, 0)),
        compiler_params=_params(1, VMEM_LIMIT),
        name="dense_ffn",
    )(x, g.reshape(1, d), wg, wu, wd)


def _moe_kernel(te_ref, nt_ref, xs_ref, wg_ref, wu_ref, wd_ref, o_ref, *, rows):
    t = pl.program_id(0)

    @pl.when(t < nt_ref[0])
    def _():
        xn = _from_token_tiles(xs_ref, rows).astype(BF16)
        mid = (jax.nn.silu(_dot(xn, wg_ref[0])) * _dot(xn, wu_ref[0])).astype(BF16)
        _to_token_tiles(o_ref, _dot(mid, wd_ref[0]))

    @pl.when(t >= nt_ref[0])
    def _():
        o_ref[...] = jnp.zeros_like(o_ref)


def moe_experts(xs, tile_expert, n_tiles, wg, wu, wd, n_grid_tiles, rows):
    def resident(w):
        return pl.BlockSpec((1,) + w.shape[1:], lambda t, te, nt: (te[t], 0, 0), pipeline_mode=pl.Buffered(1))

    tokens = pl.BlockSpec((rows * TOKEN_TILE, LANES), lambda t, te, nt: (t, 0))

    grid_spec = pltpu.PrefetchScalarGridSpec(
        num_scalar_prefetch=2,
        grid=(n_grid_tiles,),
        in_specs=[tokens, resident(wg), resident(wu), resident(wd)],
        out_specs=tokens,
    )
    return pl.pallas_call(
        functools.partial(_moe_kernel, rows=rows),
        out_shape=jax.ShapeDtypeStruct((n_grid_tiles * rows * TOKEN_TILE, LANES), F32),
        grid_spec=grid_spec,
        compiler_params=_params(1, VMEM_LIMIT),
        name="moe_experts",
    )(tile_expert, n_tiles, xs, wg, wu, wd)


def _rope_kernel(pos_ref, freq_ref, cos_ref, sin_ref):
    ang = freq_ref[...] * pos_ref[...]
    cos, sin = jnp.cos(ang), jnp.sin(ang)
    n = ang.shape[1]
    pad = HEAD_PAD - QK_NOPE - QK_ROPE
    cos_ref[...] = jnp.concatenate([jnp.ones((QK_NOPE, n), F32), cos, cos, jnp.zeros((pad, n), F32)], axis=0)
    sin_ref[...] = jnp.concatenate([jnp.zeros((QK_NOPE, n), F32), sin, sin, jnp.zeros((pad, n), F32)], axis=0)


def rope_tables(positions):
    n = positions.shape[0]
    half = QK_ROPE // 2
    inv_freq = ROPE_BASE ** (-jnp.arange(half, dtype=F32) * (2.0 / QK_ROPE))
    tile = min(n, 4096)
    return pl.pallas_call(
        _rope_kernel,
        out_shape=[jax.ShapeDtypeStruct((HEAD_PAD, n), F32)] * 2,
        grid=(n // tile,),
        in_specs=[pl.BlockSpec((1, tile), lambda i: (0, i)), pl.BlockSpec((half, 1), lambda i: (0, 0))],
        out_specs=[pl.BlockSpec((HEAD_PAD, tile), lambda i: (0, i))] * 2,
        compiler_params=_params(1),
        name="rope_tables",
    )(positions.astype(F32).reshape(1, n), inv_freq.reshape(half, 1))


def _mla_weights(w_dkv, w_ukv, w_uq):
    half = QK_ROPE // 2
    pad = HEAD_PAD - QK_NOPE - QK_ROPE
    scale = (QK_NOPE + QK_ROPE) ** -0.5
    w_dkv_p = jnp.pad(w_dkv, ((0, 0), (0, LANES - QK_ROPE)))
    ukv = w_ukv.reshape(KV_LORA, N_HEADS, QK_NOPE + V_HEAD)
    k_nope = jnp.pad(ukv[:, :, :QK_NOPE], ((0, 0), (0, 0), (0, HEAD_PAD - QK_NOPE)))
    eye = jnp.eye(half, dtype=F32)
    zer = jnp.zeros((half, half), F32)

    def place(x1_to, x2_to):
        blk = jnp.concatenate([jnp.concatenate([x1_to[0], x1_to[1]], axis=1),
                               jnp.concatenate([x2_to[0], x2_to[1]], axis=1)], axis=0)
        return jnp.pad(blk, ((0, 0), (QK_NOPE, pad)))

    rope_a = place((eye, zer), (zer, eye))
    rope_b = place((zer, eye), (-eye, zer))
    rope_a = jnp.tile(rope_a[:, None, :], (1, N_HEADS, 1))
    rope_b = jnp.tile(rope_b[:, None, :], (1, N_HEADS, 1))
    zrows = jnp.zeros((LANES - QK_ROPE, N_HEADS, HEAD_PAD), F32)
    w_ka = jnp.concatenate([k_nope, rope_a, zrows], axis=0).reshape(KV_LORA + LANES, N_HEADS * HEAD_PAD)
    w_kb = jnp.concatenate([rope_b, zrows], axis=0).reshape(LANES, N_HEADS * HEAD_PAD)
    vh = ukv[:, :, QK_NOPE:].reshape(KV_LORA, N_HEADS // 2, 2, V_HEAD)
    w_v = jnp.concatenate([jnp.pad(vh[:, :, 0], ((0, 0), (0, 0), (0, HEAD_PAD - V_HEAD))),
                           jnp.pad(vh[:, :, 1], ((0, 0), (0, 0), (HEAD_PAD - V_HEAD, 0)))], axis=2)
    w_v = w_v.reshape(KV_LORA, N_HEADS * HEAD_PAD)
    uq = w_uq.reshape(Q_LORA, N_HEADS, QK_NOPE + QK_ROPE) * (scale * LOG2_E)
    q_x1, q_x2 = uq[:, :, QK_NOPE:QK_NOPE + half], uq[:, :, QK_NOPE + half:]
    w_qa = jnp.pad(uq, ((0, 0), (0, 0), (0, pad))).reshape(Q_LORA, N_HEADS * HEAD_PAD)
    w_qb = jnp.pad(jnp.concatenate([-q_x2, q_x1], axis=2), ((0, 0), (0, 0), (QK_NOPE, pad)))
    w_qb = w_qb.reshape(Q_LORA, N_HEADS * HEAD_PAD)
    return tuple(a.astype(BF16) for a in (w_dkv_p, w_ka, w_kb, w_v, w_qa, w_qb))


def _qkv_kernel(x_ref, gkv_ref, gq_ref, wdkv_ref, glat_ref, wka_ref, wkb_ref, wv_ref, vone_ref, wdq_ref, gql_ref,
                wqa_ref, wqb_ref, cos_ref, sin_ref, q_ref, k_ref, v_ref):
    x = x_ref[...]
    xn = x * lax.rsqrt(jnp.mean(x * x, axis=-1, keepdims=True) + EPS)
    cos = jnp.tile(cos_ref[...].T, (1, N_HEADS))
    sin = jnp.tile(sin_ref[...].T, (1, N_HEADS))
    ckr = _dot((xn * gkv_ref[...]).astype(BF16), wdkv_ref[...])
    latent = _rms(ckr[:, :KV_LORA], glat_ref[...]).astype(BF16)
    rope_raw = ckr[:, KV_LORA:].astype(BF16)
    ka = _dot(jnp.concatenate([latent, rope_raw], axis=1), wka_ref[...])
    kb = _dot(rope_raw, wkb_ref[...])
    k_ref[...] = (ka * cos + kb * sin).astype(BF16)
    v_ref[...] = (_dot(latent, wv_ref[...]) + vone_ref[...]).astype(BF16)
    cq = _rms(_dot((xn * gq_ref[...]).astype(BF16), wdq_ref[...]), gql_ref[...]).astype(BF16)
    qa = _dot(cq, wqa_ref[...])
    qb = _dot(cq, wqb_ref[...])
    q_ref[...] = (qa * cos + qb * sin).astype(BF16)


def qkv_project(x, g_kv, g_q, g_lat, g_qlat, w_dq, mla_w, cos_t, sin_t, rows=256):
    n, d = x.shape
    w_dkv_p, w_ka, w_kb, w_v, w_qa, w_qb = mla_w
    hw = N_HEADS * HEAD_PAD
    lane = jnp.arange(hw, dtype=I32) % (2 * HEAD_PAD)
    v_one = ((lane == ONE_LANE_EVEN) | (lane == HEAD_PAD + ONE_LANE_ODD)).astype(F32).reshape(1, hw)

    def full(a):
        return pl.BlockSpec(a.shape, lambda i: (0,) * a.ndim)

    def row(width):
        return pl.BlockSpec((rows, width), lambda i: (i, 0))

    args = [x, g_kv.reshape(1, d), g_q.reshape(1, d), w_dkv_p, g_lat.reshape(1, KV_LORA), w_ka, w_kb, w_v, v_one,
            w_dq, g_qlat.reshape(1, Q_LORA), w_qa, w_qb, cos_t, sin_t]
    table = pl.BlockSpec((HEAD_PAD, rows), lambda i: (0, i))
    in_specs = [row(d)] + [full(a) for a in args[1:13]] + [table, table]
    return pl.pallas_call(
        _qkv_kernel,
        out_shape=[jax.ShapeDtypeStruct((n, hw), BF16)] * 3,
        grid=(n // rows,),
        in_specs=in_specs,
        out_specs=[row(hw)] * 3,
        compiler_params=_params(1, VMEM_LIMIT),
        name="qkv_project",
    )(*args)


def _attn_kernel(q_ref, k_ref, v_ref, o_ref, m_ref, acc_ref, *, tq, tk, td):
    qi = pl.program_id(2)
    m_ref[...] = jnp.full_like(m_ref, NEG_BIG)
    acc_ref[...] = jnp.zeros_like(acc_ref)

    def kv_tile(kv0, width, q0, masked):
        rows = pl.ds(pl.multiple_of(kv0, width), width)
        if masked:
            row_chunk = lax.broadcasted_iota(I32, (tq - q0, width), 0) // CHUNK
            col_chunk = lax.broadcasted_iota(I32, (tq - q0, width), 1) // CHUNK
        for h in range(2):
            cols = slice(h * HEAD_PAD, (h + 1) * HEAD_PAD)
            s = lax.dot_general(q_ref[q0:, cols], k_ref[rows, cols], (((1,), (1,)), ((), ())),
                                preferred_element_type=F32)
            if masked:
                s = jnp.where(col_chunk <= row_chunk, s, NEG_BIG)
            m_old = m_ref[h, q0:, :]
            m_new = jnp.maximum(m_old, jnp.max(s, axis=1, keepdims=True))
            p = jnp.exp2(s - jnp.tile(m_new, (1, width // LANES))).astype(BF16)
            acc_ref[h, q0:, :] = jnp.exp2(m_old - m_new) * acc_ref[h, q0:, :] + _dot(p, v_ref[rows, cols])
            m_ref[h, q0:, :] = m_new

    def full_tile(j, carry):
        kv_tile(j * tk, tk, 0, False)
        return carry

    lax.fori_loop(0, qi * (tq // tk), full_tile, 0)
    for d in range(tq // td):
        kv_tile(qi * tq + d * td, td, d * td, True)
    even, odd = acc_ref[0], acc_ref[1]
    even = even / even[:, ONE_LANE_EVEN:ONE_LANE_EVEN + 1]
    odd = odd / odd[:, ONE_LANE_ODD:ONE_LANE_ODD + 1]
    lane = lax.broadcasted_iota(I32, even.shape, 1)
    o_ref[...] = jnp.where(lane < V_HEAD, even, odd).astype(BF16)


def attention(q, k, v, batch, seq, tq=1024, tk=1024, td=512):
    tq, tk, td = min(tq, seq), min(tk, seq), min(td, seq)
    n_q = seq // tq
    pair = 2 * HEAD_PAD
    return pl.pallas_call(
        functools.partial(_attn_kernel, tq=tq, tk=tk, td=td),
        out_shape=jax.ShapeDtypeStruct((batch * seq, N_HEADS * V_HEAD), BF16),
        grid=(batch, N_HEADS // 2, n_q),
        in_specs=[
            pl.BlockSpec((tq, pair), lambda b, hp, i: (b * n_q + i, hp)),
            pl.BlockSpec((seq, pair), lambda b, hp, i: (b, hp)),
            pl.BlockSpec((seq, pair), lambda b, hp, i: (b, hp)),
        ],
        out_specs=pl.BlockSpec((tq, 2 * V_HEAD), lambda b, hp, i: (b * n_q + i, hp)),
        scratch_shapes=[pltpu.VMEM((2, tq, LANES), F32), pltpu.VMEM((2, tq, HEAD_PAD), F32)],
        compiler_params=_params(3, VMEM_LIMIT),
        name="attention",
    )(q, k, v)


ROUTE_COLS = 8
TOKEN_TILE = SUBLANES


def _to_token_tiles(ref, x):
    rows = x.shape[0]
    for k in range(x.shape[1] // LANES):
        ref[pl.ds(k, rows, stride=TOKEN_TILE), :] = x[:, k * LANES:(k + 1) * LANES]


def _from_token_tiles(ref, rows):
    return jnp.concatenate([ref[pl.ds(k, rows, stride=TOKEN_TILE), :] for k in range(TOKEN_TILE)], axis=1)


def _attn_out_router_kernel(a_ref, x_ref, w_ref, g_ref, rw_ref, tri_ref, x3_ref, h_ref, route_ref, cnt_ref,
                            carry_ref):
    @pl.when(pl.program_id(0) == 0)
    def _():
        carry_ref[...] = jnp.zeros_like(carry_ref)

    x3 = x_ref[...] + _dot(a_ref[...], w_ref[...])
    x3_ref[...] = x3
    h = _rms(x3, g_ref[...])
    _to_token_tiles(h_ref, h)
    h_hi = h.astype(BF16)
    h_lo = (h - h_hi.astype(F32)).astype(BF16)
    by_hi = _dot(h_hi, rw_ref[...])
    by_lo = _dot(h_lo, rw_ref[...])
    logits = by_hi[:, :N_EXPERTS] + by_hi[:, N_EXPERTS:] + by_lo[:, :N_EXPERTS]
    col = lax.broadcasted_iota(I32, logits.shape, 1)
    m1 = jnp.max(logits, axis=1, keepdims=True)
    i1 = jnp.min(jnp.where(logits == m1, col, N_EXPERTS), axis=1, keepdims=True)
    rest = jnp.where(col == i1, -jnp.inf, logits)
    m2 = jnp.max(rest, axis=1, keepdims=True)
    i2 = jnp.min(jnp.where(rest == m2, col, N_EXPERTS), axis=1, keepdims=True)
    e2 = jnp.exp(m2 - m1)
    w1 = 1.0 / (1.0 + e2)
    w2 = e2 / (1.0 + e2)
    sel = jnp.where((col == i1) | (col == i2), 1.0, 0.0)
    incl = _dot(tri_ref[...], sel.astype(BF16))
    carry = carry_ref[0:1, :]
    excl = incl - sel + carry
    r1 = jnp.sum(jnp.where(col == i1, excl, 0.0), axis=1, keepdims=True)
    r2 = jnp.sum(jnp.where(col == i2, excl, 0.0), axis=1, keepdims=True)
    total = carry + incl[incl.shape[0] - 1:, :]
    carry_ref[...] = jnp.broadcast_to(total, carry_ref.shape)
    cnt_ref[...] = jnp.broadcast_to(total, cnt_ref.shape)
    route = jnp.where(col == 0, i1.astype(F32), 0.0)
    for c, val in ((1, i2.astype(F32)), (2, r1), (3, r2), (4, w1), (5, w2)):
        route = jnp.where(col == c, val, route)
    route_ref[...] = route


def attn_out_router(a, x, w_o, g, router_w, rows=1024):
    n, d = x.shape
    rows = min(rows, n)
    tri = (jnp.arange(rows)[:, None] >= jnp.arange(rows)[None, :]).astype(BF16)
    rw_hi = router_w.astype(BF16)
    rw_lo = (router_w - rw_hi.astype(F32)).astype(BF16)
    router_w = jnp.concatenate([rw_hi, rw_lo], axis=1)
    return pl.pallas_call(
        _attn_out_router_kernel,
        out_shape=[jax.ShapeDtypeStruct((n, d), F32), jax.ShapeDtypeStruct((n * TOKEN_TILE, LANES), F32),
                   jax.ShapeDtypeStruct((n, ROUTE_COLS), F32), jax.ShapeDtypeStruct((8, N_EXPERTS), F32)],
        grid=(n // rows,),
        in_specs=[
            pl.BlockSpec((rows, a.shape[1]), lambda i: (i, 0)),
            pl.BlockSpec((rows, d), lambda i: (i, 0)),
            pl.BlockSpec(w_o.shape, lambda i: (0, 0)),
            pl.BlockSpec((1, d), lambda i: (0, 0)),
            pl.BlockSpec((d, 2 * N_EXPERTS), lambda i: (0, 0)),
            pl.BlockSpec((rows, rows), lambda i: (0, 0)),
        ],
        out_specs=[
            pl.BlockSpec((rows, d), lambda i: (i, 0)),
            pl.BlockSpec((rows * TOKEN_TILE, LANES), lambda i: (i, 0)),
            pl.BlockSpec((rows, ROUTE_COLS), lambda i: (i, 0)),
            pl.BlockSpec((8, N_EXPERTS), lambda i: (0, 0)),
        ],
        scratch_shapes=[pltpu.VMEM((8, N_EXPERTS), F32)],
        compiler_params=_params(1, VMEM_LIMIT),
        name="attn_out_router",
    )(a, x, w_o, g.reshape(1, d), router_w, tri)


def _tile_copy(src_ref, src_token, dst_ref, dst_token, sem):
    src = src_ref.at[pl.ds(pl.multiple_of(src_token * TOKEN_TILE, TOKEN_TILE), TOKEN_TILE)]
    dst = dst_ref.at[pl.ds(pl.multiple_of(dst_token * TOKEN_TILE, TOKEN_TILE), TOKEN_TILE)]
    return pltpu.make_async_copy(src, dst, sem)


ISSUE_UNROLL = 8


def _dispatch_kernel(zoff_ref, nt_ref, p0_ref, p1_ref, h_ref, wg_ref, wu_ref, wd_ref,
                     xs_ref, wg_out, wu_out, wd_out, zero_ref, sem, zsem, *, rows, tile_rows, n_buf_tiles):
    tile_words = tile_rows * TOKEN_TILE

    @pl.when(pl.program_id(0) == 0)
    def _():
        zero_ref[...] = jnp.zeros_like(zero_ref)
        copies = [pltpu.make_async_copy(
            zero_ref, xs_ref.at[pl.ds(pl.multiple_of(zoff_ref[e] * TOKEN_TILE, TOKEN_TILE), tile_words)], zsem)
            for e in range(N_EXPERTS)]
        for c in copies:
            c.start()
        for c in copies:
            c.wait()
        def fill_idle(t, carry):
            c = pltpu.make_async_copy(
                zero_ref, xs_ref.at[pl.ds(pl.multiple_of(t * tile_words, tile_words), tile_words)], zsem)
            c.start()
            c.wait()
            return carry

        lax.fori_loop(nt_ref[0], n_buf_tiles, fill_idle, 0)

    for src, dst in ((wg_ref, wg_out), (wu_ref, wu_out), (wd_ref, wd_out)):
        dst[...] = src[...].astype(BF16)

    def issue(r, carry):
        _tile_copy(h_ref, r, xs_ref, p0_ref[r], sem).start()
        _tile_copy(h_ref, r, xs_ref, p1_ref[r], sem).start()
        return carry

    lax.fori_loop(0, rows, issue, 0, unroll=ISSUE_UNROLL)
    for _ in range(2):
        pltpu.make_async_copy(h_ref, xs_ref.at[pl.ds(0, rows * TOKEN_TILE)], sem).wait()


def dispatch(h, pos0, pos1, zero_off, n_tiles, n_buf_tiles, tile_rows, wg, wu, wd, rows=1024):
    n = h.shape[0] // TOKEN_TILE
    rows = min(rows, n)
    n_steps = n // rows
    weights = [w.reshape(-1, w.shape[-1]) for w in (wg, wu, wd)]

    def w_spec(w):
        return pl.BlockSpec((w.shape[0] // n_steps, w.shape[1]), lambda i, z, t: (i, 0))

    grid_spec = pltpu.PrefetchScalarGridSpec(
        num_scalar_prefetch=2,
        grid=(n_steps,),
        in_specs=[
            pl.BlockSpec((rows,), lambda i, z, t: (i,), memory_space=pltpu.SMEM),
            pl.BlockSpec((rows,), lambda i, z, t: (i,), memory_space=pltpu.SMEM),
            pl.BlockSpec((rows * TOKEN_TILE, LANES), lambda i, z, t: (i, 0)),
        ] + [w_spec(w) for w in weights],
        out_specs=[pl.BlockSpec(memory_space=pl.ANY)] + [w_spec(w) for w in weights],
        scratch_shapes=[pltpu.VMEM((tile_rows * TOKEN_TILE, LANES), F32), pltpu.SemaphoreType.DMA,
                        pltpu.SemaphoreType.DMA],
    )
    xs, *cast = pl.pallas_call(
        functools.partial(_dispatch_kernel, rows=rows, tile_rows=tile_rows, n_buf_tiles=n_buf_tiles),
        out_shape=[jax.ShapeDtypeStruct((n_buf_tiles * tile_rows * TOKEN_TILE, LANES), F32)]
        + [jax.ShapeDtypeStruct(w.shape, BF16) for w in weights],
        grid_spec=grid_spec,
        compiler_params=_params(1, VMEM_LIMIT),
        name="dispatch",
    )(zero_off, n_tiles, pos0, pos1, h, *weights)
    return xs, [c.reshape(w.shape) for c, w in zip(cast, (wg, wu, wd))]


def _combine_kernel(p0_ref, p1_ref, q0_ref, q1_ref, ys_ref, x_ref, w_ref, g_ref, o_ref, buf, sem, *, rows):
    i = pl.program_id(0)
    slot = i % 2

    def issue(a_ref, b_ref, to_slot):
        def body(r, carry):
            _tile_copy(ys_ref, a_ref[r], buf.at[to_slot, 0], r, sem.at[to_slot]).start()
            _tile_copy(ys_ref, b_ref[r], buf.at[to_slot, 1], r, sem.at[to_slot]).start()
            return carry

        lax.fori_loop(0, rows, body, 0, unroll=ISSUE_UNROLL)

    @pl.when(i == 0)
    def _():
        issue(p0_ref, p1_ref, 0)

    @pl.when(i + 1 < pl.num_programs(0))
    def _():
        issue(q0_ref, q1_ref, 1 - slot)

    for k in range(2):
        pltpu.make_async_copy(ys_ref.at[pl.ds(0, rows * TOKEN_TILE)], buf.at[slot, k], sem.at[slot]).wait()
    w = w_ref[...]
    y = (x_ref[...] + w[:, 4:5] * _from_token_tiles(buf.at[slot, 0], rows)
         + w[:, 5:6] * _from_token_tiles(buf.at[slot, 1], rows))
    o_ref[...] = _rms(y, g_ref[...])


def combine(ys, pos0, pos1, x, route, g, rows=512):
    n, d = x.shape
    rows = min(rows, n)
    n_blocks = n // rows

    def cur(i):
        return (i,)

    def nxt(i):
        return (jnp.minimum(i + 1, n_blocks - 1),)

    return pl.pallas_call(
        functools.partial(_combine_kernel, rows=rows),
        out_shape=jax.ShapeDtypeStruct((n, d), F32),
        grid=(n_blocks,),
        in_specs=[
            pl.BlockSpec((rows,), cur, memory_space=pltpu.SMEM),
            pl.BlockSpec((rows,), cur, memory_space=pltpu.SMEM),
            pl.BlockSpec((rows,), nxt, memory_space=pltpu.SMEM),
            pl.BlockSpec((rows,), nxt, memory_space=pltpu.SMEM),
            pl.BlockSpec(memory_space=pl.ANY),
            pl.BlockSpec((rows, d), lambda i: (i, 0)),
            pl.BlockSpec((rows, ROUTE_COLS), lambda i: (i, 0)),
            pl.BlockSpec((1, d), lambda i: (0, 0)),
        ],
        out_specs=pl.BlockSpec((rows, d), lambda i: (i, 0)),
        scratch_shapes=[pltpu.VMEM((2, 2, rows * TOKEN_TILE, LANES), F32), pltpu.SemaphoreType.DMA((2,))],
        compiler_params=_params(1, VMEM_LIMIT),
        name="combine",
    )(pos0, pos1, pos0, pos1, ys, x, route, g.reshape(1, d))


def moe_layer(attn, x2, w_o, g_ffn, g_final, router_w, wg, wu, wd, tile_rows=512):
    n, d = x2.shape
    tile_rows = min(tile_rows, n)
    x3, h, route, counts = attn_out_router(attn, x2, w_o, g_ffn, router_w)
    counts = counts[0].astype(I32)
    tiles = (counts + tile_rows - 1) // tile_rows
    tile_start = jnp.cumsum(tiles) - tiles
    row_start = tile_start * tile_rows
    n_tiles = jnp.sum(tiles).reshape(1).astype(I32)
    max_tiles = (2 * n) // tile_rows + N_EXPERTS
    n_buf_tiles = max_tiles + 1
    e0, e1 = route[:, 0].astype(I32), route[:, 1].astype(I32)
    pos0 = row_start[e0] + route[:, 2].astype(I32)
    pos1 = row_start[e1] + route[:, 3].astype(I32)
    t_idx = jnp.arange(max_tiles, dtype=I32)
    tile_expert = jnp.sum(t_idx[:, None] >= (tile_start + tiles)[None, :], axis=1).astype(I32)
    tile_expert = jnp.where(t_idx < n_tiles[0], jnp.minimum(tile_expert, N_EXPERTS - 1),
                            tile_expert[n_tiles[0] - 1])
    zero_off = (row_start + counts).astype(I32)
    xs, (wg, wu, wd) = dispatch(h, pos0, pos1, zero_off, n_tiles, n_buf_tiles, tile_rows, wg, wu, wd)
    ys = moe_experts(xs, tile_expert, n_tiles, wg, wu, wd, max_tiles, tile_rows)
    return combine(ys, pos0, pos1, x3, route, g_final)


def kernel(x, positions, norm_mix, norm_ffn, final_norm, s5_w_in, s5_lambda_re, s5_lambda_im, s5_log_dt, s5_b_re, s5_b_im, s5_c_re, s5_c_im, s5_d, s5_w_glu, s5_w_out, kv_norm, w_dkv, kv_latent_norm, w_ukv, w_dq, q_latent_norm, w_uq, w_o, ffn_w_gate, ffn_w_up, ffn_w_down, router_w, moe_w_gate, moe_w_up, moe_w_down):
    batch, seq, d = x.shape
    n = batch * seq
    x0 = x.reshape(n, d)
    u = norm_matmul_chunked(x0, norm_mix[0], s5_w_in[0].astype(BF16), BF16, S5_T)
    s5_w = _s5_weights(s5_lambda_re[0], s5_lambda_im[0], s5_log_dt[0], s5_b_re[0], s5_b_im[0],
                       s5_c_re[0], s5_c_im[0])
    y = s5_scan(u, *s5_w, s5_d[0], batch, seq).reshape(n, d)
    x1 = s5_out(y, x0, s5_w_glu[0].astype(BF16), s5_w_out[0].astype(BF16))
    x2 = dense_ffn(x1, norm_ffn[0], ffn_w_gate[0].astype(BF16), ffn_w_up[0].astype(BF16),
                   ffn_w_down[0].astype(BF16))
    cos_tab, sin_tab = rope_tables(positions.reshape(n))
    mla_w = _mla_weights(w_dkv, w_ukv, w_uq[0])
    q, k, v = qkv_project(x2, kv_norm, norm_mix[1], kv_latent_norm, q_latent_norm[0], w_dq[0].astype(BF16),
                          mla_w, cos_tab, sin_tab)
    o = attention(q, k, v, batch, seq)
    out = moe_layer(o, x2, w_o[0].astype(BF16), norm_ffn[1], final_norm, router_w[0],
                    moe_w_gate[0], moe_w_up[0], moe_w_down[0])
    return out.reshape(batch, seq, d)
```

```python
import functools

import jax
import jax.numpy as jnp
from jax import lax
from jax.experimental import pallas as pl
from jax.experimental.pallas import tpu as pltpu

F32 = jnp.float32
BF16 = jnp.bfloat16
I32 = jnp.int32
HIGHEST = lax.Precision.HIGHEST

D_MODEL = 1024
CHUNK = 64
SSM_GROUP = 16
SSM_GROUPS = D_MODEL // SSM_GROUP
SSM_STATE = 64
N_HEADS = 16
QK_NOPE = 64
QK_ROPE = 32
V_HEAD = 64
Q_LORA = 512
KV_LORA = 256
ROPE_BASE = 10000.0
D_FF = 2688
N_EXPERTS = 8
MOE_FF = 3584
EPS = 1e-6

LANES = 128
SUBLANES = 8
HEAD_PAD = 128
S5_T = 8
S5_LANE_GROUPS = LANES // SSM_GROUP
S5_BLOCKS = D_MODEL // LANES
S5_STATE_W = S5_LANE_GROUPS * SSM_STATE
NEG_BIG = -1e30
LOG2_E = 1.4426950408889634
ONE_LANE_EVEN = V_HEAD
ONE_LANE_ODD = 0
VMEM_LIMIT = 56 * 1024 * 1024


def _params(n_axes, vmem=None):
    return pltpu.CompilerParams(dimension_semantics=("arbitrary",) * n_axes, vmem_limit_bytes=vmem)


def _rms(x, g):
    return x * lax.rsqrt(jnp.mean(x * x, axis=-1, keepdims=True) + EPS) * g


def _dot(a, b):
    return jnp.dot(a, b, preferred_element_type=F32)


def _norm_matmul_chunked_kernel(x_ref, g_ref, w_ref, o_ref, slab_ref, *, t_steps):
    h = _rms(x_ref[...], g_ref[...]).astype(BF16)
    res = _dot(h, w_ref[...])
    rows, dout = res.shape
    for k in range(dout // LANES):
        slab_ref[k] = res[:, k * LANES:(k + 1) * LANES]
    for s in range(t_steps):
        for k in range(dout // LANES):
            lanes = slice(s * dout + k * LANES, s * dout + (k + 1) * LANES)
            o_ref[:, lanes] = slab_ref[k, pl.ds(s, rows // t_steps, stride=t_steps), :].astype(o_ref.dtype)


def norm_matmul_chunked(x, g, w, out_dtype, t_steps, rows=512):
    n, din = x.shape
    dout = w.shape[1]
    return pl.pallas_call(
        functools.partial(_norm_matmul_chunked_kernel, t_steps=t_steps),
        out_shape=jax.ShapeDtypeStruct((n // t_steps, t_steps * dout), out_dtype),
        grid=(n // rows,),
        in_specs=[
            pl.BlockSpec((rows, din), lambda i: (i, 0)),
            pl.BlockSpec((1, din), lambda i: (0, 0)),
            pl.BlockSpec((din, dout), lambda i: (0, 0)),
        ],
        out_specs=pl.BlockSpec((rows // t_steps, t_steps * dout), lambda i: (i, 0)),
        scratch_shapes=[pltpu.VMEM((dout // LANES, rows, LANES), F32)],
        compiler_params=_params(1),
        name="norm_matmul_chunked",
    )(x, g.reshape(1, din), w)


def _s5_weights(lam_re, lam_im, log_dt, b_re, b_im, c_re, c_im):
    t_steps = S5_T
    lr, li = lam_re.astype(F32), lam_im.astype(F32)
    dt = jnp.exp(log_dt.astype(F32))[:, None]
    mag = jnp.exp(lr * dt)
    ab_re, ab_im = mag * jnp.cos(li * dt), mag * jnp.sin(li * dt)
    den = lr * lr + li * li
    nr, ni = ab_re - 1.0, ab_im
    coef_re = (nr * lr + ni * li) / den
    coef_im = (ni * lr - nr * li) / den
    br, bi = b_re.astype(F32), b_im.astype(F32)
    bb_re = coef_re[..., None] * br - coef_im[..., None] * bi
    bb_im = coef_re[..., None] * bi + coef_im[..., None] * br
    pr, pi = [jnp.ones_like(ab_re)], [jnp.zeros_like(ab_im)]
    for _ in range(t_steps):
        pr.append(pr[-1] * ab_re - pi[-1] * ab_im)
        pi.append(pr[-2] * ab_im + pi[-1] * ab_re)
    pw_re, pw_im = jnp.stack(pr), jnp.stack(pi)
    cr, ci = c_re.astype(F32), c_im.astype(F32)
    eye = jnp.eye(S5_LANE_GROUPS, dtype=BF16)
    nb, lg = S5_BLOCKS, S5_LANE_GROUPS

    def blocks(a):
        return a.reshape(a.shape[0], nb, lg, *a.shape[2:])

    def block_diag(a, perm, eye_axes):
        a = jnp.transpose(blocks(a), perm).astype(BF16)[..., None, :]
        shape = [1] * a.ndim
        shape[eye_axes[0]], shape[eye_axes[1]] = lg, lg
        return a * eye.reshape(shape)

    dec_re = jnp.stack([pr[t_steps - 1 - s] for s in range(t_steps)])
    dec_im = jnp.stack([pi[t_steps - 1 - s] for s in range(t_steps)])
    si_re = dec_re[..., None] * bb_re[None] - dec_im[..., None] * bb_im[None]
    si_im = dec_re[..., None] * bb_im[None] + dec_im[..., None] * bb_re[None]
    w_in = jnp.stack([block_diag(a, (1, 0, 2, 4, 3), (2, 4)) for a in (si_re, si_im)], axis=4)
    w_in = w_in.reshape(nb, t_steps * LANES, 2 * S5_STATE_W)
    up_re, up_im = pw_re[1:t_steps + 1], pw_im[1:t_steps + 1]
    so_re = cr[None] * up_re[:, :, None, :] - ci[None] * up_im[:, :, None, :]
    so_im = cr[None] * up_im[:, :, None, :] + ci[None] * up_re[:, :, None, :]
    w_out = jnp.stack([block_diag(a, (1, 2, 4, 0, 3), (1, 4)) for a in (so_re, -so_im)], axis=1)
    w_out = w_out.reshape(nb, 2 * S5_STATE_W, t_steps * LANES)
    lg_re = cr[None] * pw_re[:t_steps, :, None, :] - ci[None] * pw_im[:t_steps, :, None, :]
    lg_im = cr[None] * pw_im[:t_steps, :, None, :] + ci[None] * pw_re[:t_steps, :, None, :]
    k_lag = (jnp.einsum("jgnp,gpm->jgnm", lg_re, bb_re, precision=HIGHEST)
             - jnp.einsum("jgnp,gpm->jgnm", lg_im, bb_im, precision=HIGHEST))
    w_lag = block_diag(k_lag, (1, 0, 2, 4, 3), (2, 4)).reshape(nb, t_steps, LANES, LANES)
    zero = jnp.zeros((nb, LANES, LANES), BF16)
    pairs = []
    for d in range(t_steps // 2):
        rows = []
        for s_local in range(2):
            lag = [2 * d + t_local - s_local for t_local in range(2)]
            rows.append(jnp.concatenate([w_lag[:, j] if j >= 0 else zero for j in lag], axis=2))
        pairs.append(jnp.concatenate(rows, axis=1))
    w_pairs = jnp.stack(pairs, axis=1)
    lam_t = jnp.concatenate([blocks(pw_re[t_steps][None])[0].reshape(nb, 1, S5_STATE_W),
                             blocks(pw_im[t_steps][None])[0].reshape(nb, 1, S5_STATE_W)], axis=2)
    return w_in, w_out, w_pairs, lam_t


def _s5_kernel(*refs, batch, rows_per_batch):
    t_steps = S5_T
    u_refs = refs[:t_steps]
    w_in_ref, w_out_ref, w_pair_ref, lam_ref, d_ref, y_ref, state_ref, carry_ref = refs[t_steps:]
    n_rows = batch * rows_per_batch
    half = S5_STATE_W

    @pl.when(pl.program_id(1) == 0)
    def _():
        carry_ref[...] = jnp.zeros_like(carry_ref)

    us = [r[...].reshape(n_rows, LANES) for r in u_refs]
    u_all = jnp.concatenate(us, axis=1)
    n_slabs = 2 * half // LANES
    contrib = _dot(u_all, w_in_ref[0])
    pitch = rows_per_batch + SUBLANES
    for k in range(n_slabs):
        for b in range(batch):
            state_ref[k, b * pitch:b * pitch + rows_per_batch, :] = (
                contrib[b * rows_per_batch:(b + 1) * rows_per_batch, k * LANES:(k + 1) * LANES])

    lam = lam_ref[0]
    a_re = jnp.broadcast_to(lam[:, :half], (batch, half))
    a_im = jnp.broadcast_to(lam[:, half:], (batch, half))

    def step(c, h):
        rows = pl.ds(c, batch, stride=pitch)
        x = jnp.concatenate([state_ref[k, rows, :] for k in range(n_slabs)], axis=1)
        for k in range(n_slabs):
            state_ref[k, rows, :] = h[:, k * LANES:(k + 1) * LANES]
        h_re, h_im = h[:, :half], h[:, half:]
        n_re = a_re * h_re - a_im * h_im + x[:, :half]
        n_im = a_re * h_im + a_im * h_re + x[:, half:]
        return jnp.concatenate([n_re, n_im], axis=1)

    carry_ref[...] = lax.fori_loop(0, rows_per_batch, step, carry_ref[...])

    entering = jnp.concatenate(
        [jnp.concatenate([state_ref[k, b * pitch:b * pitch + rows_per_batch, :] for b in range(batch)], axis=0)
         for k in range(n_slabs)], axis=1).astype(BF16)
    y_state = _dot(entering, w_out_ref[0])
    d_skip = d_ref[...]
    for a in range(t_steps // 2):
        acc = y_state[:, 2 * a * LANES:(2 * a + 2) * LANES]
        for b in range(a + 1):
            acc = acc + _dot(u_all[:, 2 * b * LANES:(2 * b + 2) * LANES], w_pair_ref[0, a - b])
        for t_local in range(2):
            t = 2 * a + t_local
            y = acc[:, t_local * LANES:(t_local + 1) * LANES] + d_skip * us[t].astype(F32)
            y_ref[:, pl.ds(t, rows_per_batch, stride=t_steps), :] = (
                jax.nn.gelu(y).reshape(batch, rows_per_batch, LANES))


def s5_scan(u, w_in, w_out, w_pairs, lam_t, d_skip, batch, seq):
    t_steps = S5_T
    n_chunk_rows = seq // t_steps
    rows_per_batch = min(128, n_chunk_rows)
    u3 = u.reshape(batch, n_chunk_rows, t_steps * D_MODEL)
    blk = (batch, rows_per_batch, LANES)

    def u_spec(s):
        return pl.BlockSpec(blk, lambda j, c, s=s: (0, c, s * S5_BLOCKS + j))

    return pl.pallas_call(
        functools.partial(_s5_kernel, batch=batch, rows_per_batch=rows_per_batch),
        out_shape=jax.ShapeDtypeStruct((batch, seq, D_MODEL), F32),
        grid=(S5_BLOCKS, n_chunk_rows // rows_per_batch),
        in_specs=[u_spec(s) for s in range(t_steps)] + [
            pl.BlockSpec((1, t_steps * LANES, 2 * S5_STATE_W), lambda j, c: (j, 0, 0)),
            pl.BlockSpec((1, 2 * S5_STATE_W, t_steps * LANES), lambda j, c: (j, 0, 0)),
            pl.BlockSpec((1, t_steps // 2, 2 * LANES, 2 * LANES), lambda j, c: (j, 0, 0, 0)),
            pl.BlockSpec((1, 1, 2 * S5_STATE_W), lambda j, c: (j, 0, 0)),
            pl.BlockSpec((1, LANES), lambda j, c: (0, j)),
        ],
        out_specs=pl.BlockSpec((batch, rows_per_batch * t_steps, LANES), lambda j, c: (0, c, j)),
        scratch_shapes=[
            pltpu.VMEM((2 * S5_STATE_W // LANES, batch * (rows_per_batch + SUBLANES), LANES), F32),
            pltpu.VMEM((batch, 2 * S5_STATE_W), F32),
        ],
        compiler_params=_params(2, VMEM_LIMIT),
        name="s5_scan",
    )(*([u3] * t_steps), w_in, w_out, w_pairs, lam_t, d_skip.reshape(1, D_MODEL))


def _s5_out_kernel(y_ref, x_ref, w_glu_ref, w_out_ref, o_ref):
    y = y_ref[...]
    gate = jax.nn.sigmoid(_dot(y.astype(BF16), w_glu_ref[...]))
    z = (y * gate).astype(BF16)
    o_ref[...] = x_ref[...] + _dot(z, w_out_ref[...])


def s5_out(y, x, w_glu, w_out, rows=512):
    n, d = x.shape
    return pl.pallas_call(
        _s5_out_kernel,
        out_shape=jax.ShapeDtypeStruct((n, d), F32),
        grid=(n // rows,),
        in_specs=[
            pl.BlockSpec((rows, d), lambda i: (i, 0)),
            pl.BlockSpec((rows, d), lambda i: (i, 0)),
            pl.BlockSpec((d, d), lambda i: (0, 0)),
            pl.BlockSpec((d, d), lambda i: (0, 0)),
        ],
        out_specs=pl.BlockSpec((rows, d), lambda i: (i, 0)),
        compiler_params=_params(1),
        name="s5_out",
    )(y, x, w_glu, w_out)


def _ffn_kernel(x_ref, g_ref, wg_ref, wu_ref, wd_ref, o_ref):
    x = x_ref[...]
    xn = _rms(x, g_ref[...]).astype(BF16)
    mid = (jax.nn.silu(_dot(xn, wg_ref[...])) * _dot(xn, wu_ref[...])).astype(BF16)
    o_ref[...] = x + _dot(mid, wd_ref[...])


def dense_ffn(x, g, wg, wu, wd, rows=512):
    n, d = x.shape

    def resident(a):
        return pl.BlockSpec(a.shape, lambda i: (0, 0), pipeline_mode=pl.Buffered(1))

    return pl.pallas_call(
        _ffn_kernel,
        out_shape=jax.ShapeDtypeStruct((n, d), F32),
        grid=(n // rows,),
        in_specs=[
            pl.BlockSpec((rows, d), lambda i: (i, 0)),
            pl.BlockSpec((1, d), lambda i: (0, 0)),
            resident(wg), resident(wu), resident(wd),
        ],
        out_specs=pl.BlockSpec((rows, d), lambda i: (i, 0)),
        compiler_params=_params(1, VMEM_LIMIT),
        name="dense_ffn",
    )(x, g.reshape(1, d), wg, wu, wd)


def _moe_kernel(te_ref, nt_ref, xs_ref, wg_ref, wu_ref, wd_ref, o_ref, *, rows):
    t = pl.program_id(0)

    @pl.when(t < nt_ref[0])
    def _():
        xn = _from_token_tiles(xs_ref, rows).astype(BF16)
        mid = (jax.nn.silu(_dot(xn, wg_ref[0])) * _dot(xn, wu_ref[0])).astype(BF16)
        _to_token_tiles(o_ref, _dot(mid, wd_ref[0]))

    @pl.when(t >= nt_ref[0])
    def _():
        o_ref[...] = jnp.zeros_like(o_ref)


def moe_experts(xs, tile_expert, n_tiles, wg, wu, wd, n_grid_tiles, rows):
    def resident(w):
        return pl.BlockSpec((1,) + w.shape[1:], lambda t, te, nt: (te[t], 0, 0), pipeline_mode=pl.Buffered(1))

    tokens = pl.BlockSpec((rows * TOKEN_TILE, LANES), lambda t, te, nt: (t, 0))

    grid_spec = pltpu.PrefetchScalarGridSpec(
        num_scalar_prefetch=2,
        grid=(n_grid_tiles,),
        in_specs=[tokens, resident(wg), resident(wu), resident(wd)],
        out_specs=tokens,
    )
    return pl.pallas_call(
        functools.partial(_moe_kernel, rows=rows),
        out_shape=jax.ShapeDtypeStruct((n_grid_tiles * rows * TOKEN_TILE, LANES), F32),
        grid_spec=grid_spec,
        compiler_params=_params(1, VMEM_LIMIT),
        name="moe_experts",
    )(tile_expert, n_tiles, xs, wg, wu, wd)


def _rope_kernel(pos_ref, freq_ref, cos_ref, sin_ref):
    ang = freq_ref[...] * pos_ref[...]
    cos, sin = jnp.cos(ang), jnp.sin(ang)
    n = ang.shape[1]
    pad = HEAD_PAD - QK_NOPE - QK_ROPE
    cos_ref[...] = jnp.concatenate([jnp.ones((QK_NOPE, n), F32), cos, cos, jnp.zeros((pad, n), F32)], axis=0)
    sin_ref[...] = jnp.concatenate([jnp.zeros((QK_NOPE, n), F32), sin, sin, jnp.zeros((pad, n), F32)], axis=0)


def rope_tables(positions):
    n = positions.shape[0]
    half = QK_ROPE // 2
    inv_freq = ROPE_BASE ** (-jnp.arange(half, dtype=F32) * (2.0 / QK_ROPE))
    tile = min(n, 4096)
    return pl.pallas_call(
        _rope_kernel,
        out_shape=[jax.ShapeDtypeStruct((HEAD_PAD, n), F32)] * 2,
        grid=(n // tile,),
        in_specs=[pl.BlockSpec((1, tile), lambda i: (0, i)), pl.BlockSpec((half, 1), lambda i: (0, 0))],
        out_specs=[pl.BlockSpec((HEAD_PAD, tile), lambda i: (0, i))] * 2,
        compiler_params=_params(1),
        name="rope_tables",
    )(positions.astype(F32).reshape(1, n), inv_freq.reshape(half, 1))


def _mla_weights(w_dkv, w_ukv, w_uq):
    half = QK_ROPE // 2
    pad = HEAD_PAD - QK_NOPE - QK_ROPE
    scale = (QK_NOPE + QK_ROPE) ** -0.5
    w_dkv_p = jnp.pad(w_dkv, ((0, 0), (0, LANES - QK_ROPE)))
    ukv = w_ukv.reshape(KV_LORA, N_HEADS, QK_NOPE + V_HEAD)
    k_nope = jnp.pad(ukv[:, :, :QK_NOPE], ((0, 0), (0, 0), (0, HEAD_PAD - QK_NOPE)))
    eye = jnp.eye(half, dtype=F32)
    zer = jnp.zeros((half, half), F32)

    def place(x1_to, x2_to):
        blk = jnp.concatenate([jnp.concatenate([x1_to[0], x1_to[1]], axis=1),
                               jnp.concatenate([x2_to[0], x2_to[1]], axis=1)], axis=0)
        return jnp.pad(blk, ((0, 0), (QK_NOPE, pad)))

    rope_a = place((eye, zer), (zer, eye))
    rope_b = place((zer, eye), (-eye, zer))
    rope_a = jnp.tile(rope_a[:, None, :], (1, N_HEADS, 1))
    rope_b = jnp.tile(rope_b[:, None, :], (1, N_HEADS, 1))
    zrows = jnp.zeros((LANES - QK_ROPE, N_HEADS, HEAD_PAD), F32)
    w_ka = jnp.concatenate([k_nope, rope_a, zrows], axis=0).reshape(KV_LORA + LANES, N_HEADS * HEAD_PAD)
    w_kb = jnp.concatenate([rope_b, zrows], axis=0).reshape(LANES, N_HEADS * HEAD_PAD)
    vh = ukv[:, :, QK_NOPE:].reshape(KV_LORA, N_HEADS // 2, 2, V_HEAD)
    w_v = jnp.concatenate([jnp.pad(vh[:, :, 0], ((0, 0), (0, 0), (0, HEAD_PAD - V_HEAD))),
                           jnp.pad(vh[:, :, 1], ((0, 0), (0, 0), (HEAD_PAD - V_HEAD, 0)))], axis=2)
    w_v = w_v.reshape(KV_LORA, N_HEADS * HEAD_PAD)
    uq = w_uq.reshape(Q_LORA, N_HEADS, QK_NOPE + QK_ROPE) * (scale * LOG2_E)
    q_x1, q_x2 = uq[:, :, QK_NOPE:QK_NOPE + half], uq[:, :, QK_NOPE + half:]
    w_qa = jnp.pad(uq, ((0, 0), (0, 0), (0, pad))).reshape(Q_LORA, N_HEADS * HEAD_PAD)
    w_qb = jnp.pad(jnp.concatenate([-q_x2, q_x1], axis=2), ((0, 0), (0, 0), (QK_NOPE, pad)))
    w_qb = w_qb.reshape(Q_LORA, N_HEADS * HEAD_PAD)
    return tuple(a.astype(BF16) for a in (w_dkv_p, w_ka, w_kb, w_v, w_qa, w_qb))


def _qkv_kernel(x_ref, gkv_ref, gq_ref, wdkv_ref, glat_ref, wka_ref, wkb_ref, wv_ref, vone_ref, wdq_ref, gql_ref,
                wqa_ref, wqb_ref, cos_ref, sin_ref, q_ref, k_ref, v_ref):
    x = x_ref[...]
    xn = x * lax.rsqrt(jnp.mean(x * x, axis=-1, keepdims=True) + EPS)
    cos = jnp.tile(cos_ref[...].T, (1, N_HEADS))
    sin = jnp.tile(sin_ref[...].T, (1, N_HEADS))
    ckr = _dot((xn * gkv_ref[...]).astype(BF16), wdkv_ref[...])
    latent = _rms(ckr[:, :KV_LORA], glat_ref[...]).astype(BF16)
    rope_raw = ckr[:, KV_LORA:].astype(BF16)
    ka = _dot(jnp.concatenate([latent, rope_raw], axis=1), wka_ref[...])
    kb = _dot(rope_raw, wkb_ref[...])
    k_ref[...] = (ka * cos + kb * sin).astype(BF16)
    v_ref[...] = (_dot(latent, wv_ref[...]) + vone_ref[...]).astype(BF16)
    cq = _rms(_dot((xn * gq_ref[...]).astype(BF16), wdq_ref[...]), gql_ref[...]).astype(BF16)
    qa = _dot(cq, wqa_ref[...])
    qb = _dot(cq, wqb_ref[...])
    q_ref[...] = (qa * cos + qb * sin).astype(BF16)


def qkv_project(x, g_kv, g_q, g_lat, g_qlat, w_dq, mla_w, cos_t, sin_t, rows=512):
    n, d = x.shape
    w_dkv_p, w_ka, w_kb, w_v, w_qa, w_qb = mla_w
    hw = N_HEADS * HEAD_PAD
    lane = jnp.arange(hw, dtype=I32) % (2 * HEAD_PAD)
    v_one = ((lane == ONE_LANE_EVEN) | (lane == HEAD_PAD + ONE_LANE_ODD)).astype(F32).reshape(1, hw)

    def full(a):
        return pl.BlockSpec(a.shape, lambda i: (0,) * a.ndim, pipeline_mode=pl.Buffered(1))

    def row(width):
        return pl.BlockSpec((rows, width), lambda i: (i, 0))

    args = [x, g_kv.reshape(1, d), g_q.reshape(1, d), w_dkv_p, g_lat.reshape(1, KV_LORA), w_ka, w_kb, w_v, v_one,
            w_dq, g_qlat.reshape(1, Q_LORA), w_qa, w_qb, cos_t, sin_t]
    table = pl.BlockSpec((HEAD_PAD, rows), lambda i: (0, i))
    in_specs = [row(d)] + [full(a) for a in args[1:13]] + [table, table]
    return pl.pallas_call(
        _qkv_kernel,
        out_shape=[jax.ShapeDtypeStruct((n, hw), BF16)] * 3,
        grid=(n // rows,),
        in_specs=in_specs,
        out_specs=[row(hw)] * 3,
        compiler_params=_params(1, VMEM_LIMIT),
        name="qkv_project",
    )(*args)


def _attn_kernel(q_ref, k_ref, v_ref, o_ref, m_ref, acc_ref, *, tq, tk, td):
    qi = pl.program_id(2)
    m_ref[...] = jnp.full_like(m_ref, NEG_BIG)
    acc_ref[...] = jnp.zeros_like(acc_ref)

    def kv_tile(kv0, width, q0, masked):
        rows = pl.ds(pl.multiple_of(kv0, width), width)
        if masked:
            row_chunk = lax.broadcasted_iota(I32, (tq - q0, width), 0) // CHUNK
            col_chunk = lax.broadcasted_iota(I32, (tq - q0, width), 1) // CHUNK
        for h in range(2):
            cols = slice(h * HEAD_PAD, (h + 1) * HEAD_PAD)
            s = lax.dot_general(q_ref[q0:, cols], k_ref[rows, cols], (((1,), (1,)), ((), ())),
                                preferred_element_type=F32)
            if masked:
                s = jnp.where(col_chunk <= row_chunk, s, NEG_BIG)
            m_old = m_ref[h, q0:, :]
            m_new = jnp.maximum(m_old, jnp.max(s, axis=1, keepdims=True))
            p = jnp.exp2(s - jnp.tile(m_new, (1, width // LANES))).astype(BF16)
            acc_ref[h, q0:, :] = jnp.exp2(m_old - m_new) * acc_ref[h, q0:, :] + _dot(p, v_ref[rows, cols])
            m_ref[h, q0:, :] = m_new

    def full_tile(j, carry):
        kv_tile(j * tk, tk, 0, False)
        return carry

    lax.fori_loop(0, qi * (tq // tk), full_tile, 0)
    for d in range(tq // td):
        kv_tile(qi * tq + d * td, td, d * td, True)
    even, odd = acc_ref[0], acc_ref[1]
    even = even / even[:, ONE_LANE_EVEN:ONE_LANE_EVEN + 1]
    odd = odd / odd[:, ONE_LANE_ODD:ONE_LANE_ODD + 1]
    lane = lax.broadcasted_iota(I32, even.shape, 1)
    o_ref[...] = jnp.where(lane < V_HEAD, even, odd).astype(BF16)


def attention(q, k, v, batch, seq, tq=1024, tk=1024, td=512):
    tq, tk, td = min(tq, seq), min(tk, seq), min(td, seq)
    n_q = seq // tq
    pair = 2 * HEAD_PAD
    return pl.pallas_call(
        functools.partial(_attn_kernel, tq=tq, tk=tk, td=td),
        out_shape=jax.ShapeDtypeStruct((batch * seq, N_HEADS * V_HEAD), BF16),
        grid=(batch, N_HEADS // 2, n_q),
        in_specs=[
            pl.BlockSpec((tq, pair), lambda b, hp, i: (b * n_q + i, hp)),
            pl.BlockSpec((seq, pair), lambda b, hp, i: (b, hp)),
            pl.BlockSpec((seq, pair), lambda b, hp, i: (b, hp)),
        ],
        out_specs=pl.BlockSpec((tq, 2 * V_HEAD), lambda b, hp, i: (b * n_q + i, hp)),
        scratch_shapes=[pltpu.VMEM((2, tq, LANES), F32), pltpu.VMEM((2, tq, HEAD_PAD), F32)],
        compiler_params=_params(3, VMEM_LIMIT),
        name="attention",
    )(q, k, v)


ROUTE_COLS = 8
TOKEN_TILE = SUBLANES


def _to_token_tiles(ref, x):
    rows = x.shape[0]
    for k in range(x.shape[1] // LANES):
        ref[pl.ds(k, rows, stride=TOKEN_TILE), :] = x[:, k * LANES:(k + 1) * LANES]


def _from_token_tiles(ref, rows):
    return jnp.concatenate([ref[pl.ds(k, rows, stride=TOKEN_TILE), :] for k in range(TOKEN_TILE)], axis=1)


def _attn_out_router_kernel(a_ref, x_ref, w_ref, g_ref, rw_ref, tri_ref, x3_ref, h_ref, route_ref, route_t_ref,
                            cnt_ref, carry_ref):
    @pl.when(pl.program_id(0) == 0)
    def _():
        carry_ref[...] = jnp.zeros_like(carry_ref)

    x3 = x_ref[...] + _dot(a_ref[...], w_ref[...])
    x3_ref[...] = x3
    h = _rms(x3, g_ref[...])
    _to_token_tiles(h_ref, h)
    h_hi = h.astype(BF16)
    h_lo = (h - h_hi.astype(F32)).astype(BF16)
    by_hi = _dot(h_hi, rw_ref[...])
    by_lo = _dot(h_lo, rw_ref[...])
    logits = by_hi[:, :N_EXPERTS] + by_hi[:, N_EXPERTS:] + by_lo[:, :N_EXPERTS]
    col = lax.broadcasted_iota(I32, logits.shape, 1)
    m1 = jnp.max(logits, axis=1, keepdims=True)
    i1 = jnp.min(jnp.where(logits == m1, col, N_EXPERTS), axis=1, keepdims=True)
    rest = jnp.where(col == i1, -jnp.inf, logits)
    m2 = jnp.max(rest, axis=1, keepdims=True)
    i2 = jnp.min(jnp.where(rest == m2, col, N_EXPERTS), axis=1, keepdims=True)
    e2 = jnp.exp(m2 - m1)
    w1 = 1.0 / (1.0 + e2)
    w2 = e2 / (1.0 + e2)
    sel = jnp.where((col == i1) | (col == i2), 1.0, 0.0)
    incl = _dot(tri_ref[...], sel.astype(BF16))
    carry = carry_ref[0:1, :]
    excl = incl - sel + carry
    r1 = jnp.sum(jnp.where(col == i1, excl, 0.0), axis=1, keepdims=True)
    r2 = jnp.sum(jnp.where(col == i2, excl, 0.0), axis=1, keepdims=True)
    total = carry + incl[incl.shape[0] - 1:, :]
    carry_ref[...] = jnp.broadcast_to(total, carry_ref.shape)
    cnt_ref[...] = jnp.broadcast_to(total, cnt_ref.shape)
    route = jnp.where(col == 0, i1.astype(F32), 0.0)
    for c, val in ((1, i2.astype(F32)), (2, r1), (3, r2), (4, w1), (5, w2)):
        route = jnp.where(col == c, val, route)
    route_ref[...] = route
    eye = (lax.broadcasted_iota(I32, (ROUTE_COLS, ROUTE_COLS), 0)
           == lax.broadcasted_iota(I32, (ROUTE_COLS, ROUTE_COLS), 1)).astype(F32)
    route_t_ref[...] = lax.dot_general(eye, route, (((1,), (1,)), ((), ())), precision=HIGHEST,
                                       preferred_element_type=F32)


def attn_out_router(a, x, w_o, g, router_w, rows=1024):
    n, d = x.shape
    rows = min(rows, n)
    tri = (jnp.arange(rows)[:, None] >= jnp.arange(rows)[None, :]).astype(BF16)
    rw_hi = router_w.astype(BF16)
    rw_lo = (router_w - rw_hi.astype(F32)).astype(BF16)
    router_w = jnp.concatenate([rw_hi, rw_lo], axis=1)
    return pl.pallas_call(
        _attn_out_router_kernel,
        out_shape=[jax.ShapeDtypeStruct((n, d), F32), jax.ShapeDtypeStruct((n * TOKEN_TILE, LANES), F32),
                   jax.ShapeDtypeStruct((n, ROUTE_COLS), F32), jax.ShapeDtypeStruct((ROUTE_COLS, n), F32),
                   jax.ShapeDtypeStruct((8, N_EXPERTS), F32)],
        grid=(n // rows,),
        in_specs=[
            pl.BlockSpec((rows, a.shape[1]), lambda i: (i, 0)),
            pl.BlockSpec((rows, d), lambda i: (i, 0)),
            pl.BlockSpec(w_o.shape, lambda i: (0, 0)),
            pl.BlockSpec((1, d), lambda i: (0, 0)),
            pl.BlockSpec((d, 2 * N_EXPERTS), lambda i: (0, 0)),
            pl.BlockSpec((rows, rows), lambda i: (0, 0)),
        ],
        out_specs=[
            pl.BlockSpec((rows, d), lambda i: (i, 0)),
            pl.BlockSpec((rows * TOKEN_TILE, LANES), lambda i: (i, 0)),
            pl.BlockSpec((rows, ROUTE_COLS), lambda i: (i, 0)),
            pl.BlockSpec((ROUTE_COLS, rows), lambda i: (0, i)),
            pl.BlockSpec((8, N_EXPERTS), lambda i: (0, 0)),
        ],
        scratch_shapes=[pltpu.VMEM((8, N_EXPERTS), F32)],
        compiler_params=_params(1, VMEM_LIMIT),
        name="attn_out_router",
    )(a, x, w_o, g.reshape(1, d), router_w, tri)


def _tile_copy(src_ref, src_token, dst_ref, dst_token, sem):
    src = src_ref.at[pl.ds(pl.multiple_of(src_token * TOKEN_TILE, TOKEN_TILE), TOKEN_TILE)]
    dst = dst_ref.at[pl.ds(pl.multiple_of(dst_token * TOKEN_TILE, TOKEN_TILE), TOKEN_TILE)]
    return pltpu.make_async_copy(src, dst, sem)


ISSUE_UNROLL = 8


def _dispatch_kernel(zoff_ref, nt_ref, p0_ref, p1_ref, h_ref, wg_ref, wu_ref, wd_ref,
                     xs_ref, wg_out, wu_out, wd_out, zero_ref, sem, zsem, *, rows, tile_rows, n_buf_tiles):
    tile_words = tile_rows * TOKEN_TILE

    @pl.when(pl.program_id(0) == 0)
    def _():
        zero_ref[...] = jnp.zeros_like(zero_ref)
        copies = [pltpu.make_async_copy(
            zero_ref, xs_ref.at[pl.ds(pl.multiple_of(zoff_ref[e] * TOKEN_TILE, TOKEN_TILE), tile_words)], zsem)
            for e in range(N_EXPERTS)]
        for c in copies:
            c.start()
        for c in copies:
            c.wait()
        def fill_idle(t, carry):
            c = pltpu.make_async_copy(
                zero_ref, xs_ref.at[pl.ds(pl.multiple_of(t * tile_words, tile_words), tile_words)], zsem)
            c.start()
            c.wait()
            return carry

        lax.fori_loop(nt_ref[0], n_buf_tiles, fill_idle, 0)

    for src, dst in ((wg_ref, wg_out), (wu_ref, wu_out), (wd_ref, wd_out)):
        dst[...] = src[...].astype(BF16)

    def issue(r, carry):
        _tile_copy(h_ref, r, xs_ref, p0_ref[r], sem).start()
        _tile_copy(h_ref, r, xs_ref, p1_ref[r], sem).start()
        return carry

    lax.fori_loop(0, rows, issue, 0, unroll=ISSUE_UNROLL)
    for _ in range(2):
        pltpu.make_async_copy(h_ref, xs_ref.at[pl.ds(0, rows * TOKEN_TILE)], sem).wait()


def dispatch(h, pos0, pos1, zero_off, n_tiles, n_buf_tiles, tile_rows, wg, wu, wd, rows=1024):
    n = h.shape[0] // TOKEN_TILE
    rows = min(rows, n)
    n_steps = n // rows
    weights = [w.reshape(-1, w.shape[-1]) for w in (wg, wu, wd)]

    def w_spec(w):
        return pl.BlockSpec((w.shape[0] // n_steps, w.shape[1]), lambda i, z, t: (i, 0))

    grid_spec = pltpu.PrefetchScalarGridSpec(
        num_scalar_prefetch=2,
        grid=(n_steps,),
        in_specs=[
            pl.BlockSpec((rows,), lambda i, z, t: (i,), memory_space=pltpu.SMEM),
            pl.BlockSpec((rows,), lambda i, z, t: (i,), memory_space=pltpu.SMEM),
            pl.BlockSpec((rows * TOKEN_TILE, LANES), lambda i, z, t: (i, 0)),
        ] + [w_spec(w) for w in weights],
        out_specs=[pl.BlockSpec(memory_space=pl.ANY)] + [w_spec(w) for w in weights],
        scratch_shapes=[pltpu.VMEM((tile_rows * TOKEN_TILE, LANES), F32), pltpu.SemaphoreType.DMA,
                        pltpu.SemaphoreType.DMA],
    )
    xs, *cast = pl.pallas_call(
        functools.partial(_dispatch_kernel, rows=rows, tile_rows=tile_rows, n_buf_tiles=n_buf_tiles),
        out_shape=[jax.ShapeDtypeStruct((n_buf_tiles * tile_rows * TOKEN_TILE, LANES), F32)]
        + [jax.ShapeDtypeStruct(w.shape, BF16) for w in weights],
        grid_spec=grid_spec,
        compiler_params=_params(1, VMEM_LIMIT),
        name="dispatch",
    )(zero_off, n_tiles, pos0, pos1, h, *weights)
    return xs, [c.reshape(w.shape) for c, w in zip(cast, (wg, wu, wd))]


def _combine_kernel(p0_ref, p1_ref, q0_ref, q1_ref, ys_ref, x_ref, w_ref, g_ref, o_ref, buf, sem, *, rows):
    i = pl.program_id(0)
    slot = i % 2

    def issue(a_ref, b_ref, to_slot):
        def body(r, carry):
            _tile_copy(ys_ref, a_ref[r], buf.at[to_slot, 0], r, sem.at[to_slot]).start()
            _tile_copy(ys_ref, b_ref[r], buf.at[to_slot, 1], r, sem.at[to_slot]).start()
            return carry

        lax.fori_loop(0, rows, body, 0, unroll=ISSUE_UNROLL)

    @pl.when(i == 0)
    def _():
        issue(p0_ref, p1_ref, 0)

    @pl.when(i + 1 < pl.num_programs(0))
    def _():
        issue(q0_ref, q1_ref, 1 - slot)

    for k in range(2):
        pltpu.make_async_copy(ys_ref.at[pl.ds(0, rows * TOKEN_TILE)], buf.at[slot, k], sem.at[slot]).wait()
    w = w_ref[...]
    y = (x_ref[...] + w[:, 4:5] * _from_token_tiles(buf.at[slot, 0], rows)
         + w[:, 5:6] * _from_token_tiles(buf.at[slot, 1], rows))
    o_ref[...] = _rms(y, g_ref[...])


def combine(ys, pos0, pos1, x, route, g, rows=512):
    n, d = x.shape
    rows = min(rows, n)
    n_blocks = n // rows

    def cur(i):
        return (i,)

    def nxt(i):
        return (jnp.minimum(i + 1, n_blocks - 1),)

    return pl.pallas_call(
        functools.partial(_combine_kernel, rows=rows),
        out_shape=jax.ShapeDtypeStruct((n, d), F32),
        grid=(n_blocks,),
        in_specs=[
            pl.BlockSpec((rows,), cur, memory_space=pltpu.SMEM),
            pl.BlockSpec((rows,), cur, memory_space=pltpu.SMEM),
            pl.BlockSpec((rows,), nxt, memory_space=pltpu.SMEM),
            pl.BlockSpec((rows,), nxt, memory_space=pltpu.SMEM),
            pl.BlockSpec(memory_space=pl.ANY),
            pl.BlockSpec((rows, d), lambda i: (i, 0)),
            pl.BlockSpec((rows, ROUTE_COLS), lambda i: (i, 0)),
            pl.BlockSpec((1, d), lambda i: (0, 0)),
        ],
        out_specs=pl.BlockSpec((rows, d), lambda i: (i, 0)),
        scratch_shapes=[pltpu.VMEM((2, 2, rows * TOKEN_TILE, LANES), F32), pltpu.SemaphoreType.DMA((2,))],
        compiler_params=_params(1, VMEM_LIMIT),
        name="combine",
    )(pos0, pos1, pos0, pos1, ys, x, route, g.reshape(1, d))


def moe_layer(attn, x2, w_o, g_ffn, g_final, router_w, wg, wu, wd, tile_rows=512):
    n, d = x2.shape
    tile_rows = min(tile_rows, n)
    x3, h, route, route_t, counts = attn_out_router(attn, x2, w_o, g_ffn, router_w)
    counts = counts[0].astype(I32)
    tiles = (counts + tile_rows - 1) // tile_rows
    tile_start = jnp.cumsum(tiles) - tiles
    row_start = tile_start * tile_rows
    n_tiles = jnp.sum(tiles).reshape(1).astype(I32)
    max_tiles = (2 * n) // tile_rows + N_EXPERTS
    n_buf_tiles = max_tiles + 1
    e0, e1, rank0, rank1 = (route_t[c].astype(I32) for c in range(4))
    pos0 = row_start[e0] + rank0
    pos1 = row_start[e1] + rank1
    t_idx = jnp.arange(max_tiles, dtype=I32)
    tile_expert = jnp.sum(t_idx[:, None] >= (tile_start + tiles)[None, :], axis=1).astype(I32)
    tile_expert = jnp.where(t_idx < n_tiles[0], jnp.minimum(tile_expert, N_EXPERTS - 1),
                            tile_expert[n_tiles[0] - 1])
    zero_off = (row_start + counts).astype(I32)
    xs, (wg, wu, wd) = dispatch(h, pos0, pos1, zero_off, n_tiles, n_buf_tiles, tile_rows, wg, wu, wd)
    ys = moe_experts(xs, tile_expert, n_tiles, wg, wu, wd, max_tiles, tile_rows)
    return combine(ys, pos0, pos1, x3, route, g_final)


def kernel(x, positions, norm_mix, norm_ffn, final_norm, s5_w_in, s5_lambda_re, s5_lambda_im, s5_log_dt, s5_b_re, s5_b_im, s5_c_re, s5_c_im, s5_d, s5_w_glu, s5_w_out, kv_norm, w_dkv, kv_latent_norm, w_ukv, w_dq, q_latent_norm, w_uq, w_o, ffn_w_gate, ffn_w_up, ffn_w_down, router_w, moe_w_gate, moe_w_up, moe_w_down):
    batch, seq, d = x.shape
    n = batch * seq
    x0 = x.reshape(n, d)
    u = norm_matmul_chunked(x0, norm_mix[0], s5_w_in[0].astype(BF16), BF16, S5_T)
    s5_w = _s5_weights(s5_lambda_re[0], s5_lambda_im[0], s5_log_dt[0], s5_b_re[0], s5_b_im[0],
                       s5_c_re[0], s5_c_im[0])
    y = s5_scan(u, *s5_w, s5_d[0], batch, seq).reshape(n, d)
    x1 = s5_out(y, x0, s5_w_glu[0].astype(BF16), s5_w_out[0].astype(BF16))
    x2 = dense_ffn(x1, norm_ffn[0], ffn_w_gate[0].astype(BF16), ffn_w_up[0].astype(BF16),
                   ffn_w_down[0].astype(BF16))
    cos_tab, sin_tab = rope_tables(positions.reshape(n))
    mla_w = _mla_weights(w_dkv, w_ukv, w_uq[0])
    q, k, v = qkv_project(x2, kv_norm, norm_mix[1], kv_latent_norm, q_latent_norm[0], w_dq[0].astype(BF16),
                          mla_w, cos_tab, sin_tab)
    o = attention(q, k, v, batch, seq)
    out = moe_layer(o, x2, w_o[0].astype(BF16), norm_ffn[1], final_norm, router_w[0],
                    moe_w_gate[0], moe_w_up[0], moe_w_down[0])
    return out.reshape(batch, seq, d)
```

```python
import functools

import jax
import jax.numpy as jnp
from jax import lax
from jax.experimental import pallas as pl
from jax.experimental.pallas import tpu as pltpu

F32 = jnp.float32
BF16 = jnp.bfloat16
I32 = jnp.int32
HIGHEST = lax.Precision.HIGHEST

D_MODEL = 1024
CHUNK = 64
SSM_GROUP = 16
SSM_GROUPS = D_MODEL // SSM_GROUP
SSM_STATE = 64
N_HEADS = 16
QK_NOPE = 64
QK_ROPE = 32
V_HEAD = 64
Q_LORA = 512
KV_LORA = 256
ROPE_BASE = 10000.0
D_FF = 2688
N_EXPERTS = 8
MOE_FF = 3584
EPS = 1e-6

LANES = 128
SUBLANES = 8
HEAD_PAD = 128
S5_T = 8
S5_LANE_GROUPS = LANES // SSM_GROUP
S5_BLOCKS = D_MODEL // LANES
S5_STATE_W = S5_LANE_GROUPS * SSM_STATE
NEG_BIG = -1e30
LOG2_E = 1.4426950408889634
ONE_LANE_EVEN = V_HEAD
ONE_LANE_ODD = 0
VMEM_LIMIT = 56 * 1024 * 1024


def _params(n_axes, vmem=None):
    return pltpu.CompilerParams(dimension_semantics=("arbitrary",) * n_axes, vmem_limit_bytes=vmem)


def _rms(x, g):
    return x * lax.rsqrt(jnp.mean(x * x, axis=-1, keepdims=True) + EPS) * g


def _dot(a, b):
    return jnp.dot(a, b, preferred_element_type=F32)


def _norm_matmul_chunked_kernel(x_ref, g_ref, w_ref, o_ref, slab_ref, *, t_steps):
    h = _rms(x_ref[...], g_ref[...]).astype(BF16)
    res = _dot(h, w_ref[...])
    rows, dout = res.shape
    for k in range(dout // LANES):
        slab_ref[k] = res[:, k * LANES:(k + 1) * LANES]
    for s in range(t_steps):
        for k in range(dout // LANES):
            lanes = slice(s * dout + k * LANES, s * dout + (k + 1) * LANES)
            o_ref[:, lanes] = slab_ref[k, pl.ds(s, rows // t_steps, stride=t_steps), :].astype(o_ref.dtype)


def norm_matmul_chunked(x, g, w, out_dtype, t_steps, rows=512):
    n, din = x.shape
    dout = w.shape[1]
    return pl.pallas_call(
        functools.partial(_norm_matmul_chunked_kernel, t_steps=t_steps),
        out_shape=jax.ShapeDtypeStruct((n // t_steps, t_steps * dout), out_dtype),
        grid=(n // rows,),
        in_specs=[
            pl.BlockSpec((rows, din), lambda i: (i, 0)),
            pl.BlockSpec((1, din), lambda i: (0, 0)),
            pl.BlockSpec((din, dout), lambda i: (0, 0)),
        ],
        out_specs=pl.BlockSpec((rows // t_steps, t_steps * dout), lambda i: (i, 0)),
        scratch_shapes=[pltpu.VMEM((dout // LANES, rows, LANES), F32)],
        compiler_params=_params(1),
        name="norm_matmul_chunked",
    )(x, g.reshape(1, din), w)


def _s5_weights(lam_re, lam_im, log_dt, b_re, b_im, c_re, c_im):
    t_steps = S5_T
    lr, li = lam_re.astype(F32), lam_im.astype(F32)
    dt = jnp.exp(log_dt.astype(F32))[:, None]
    mag = jnp.exp(lr * dt)
    ab_re, ab_im = mag * jnp.cos(li * dt), mag * jnp.sin(li * dt)
    den = lr * lr + li * li
    nr, ni = ab_re - 1.0, ab_im
    coef_re = (nr * lr + ni * li) / den
    coef_im = (ni * lr - nr * li) / den
    br, bi = b_re.astype(F32), b_im.astype(F32)
    bb_re = coef_re[..., None] * br - coef_im[..., None] * bi
    bb_im = coef_re[..., None] * bi + coef_im[..., None] * br
    pr, pi = [jnp.ones_like(ab_re)], [jnp.zeros_like(ab_im)]
    for _ in range(t_steps):
        pr.append(pr[-1] * ab_re - pi[-1] * ab_im)
        pi.append(pr[-2] * ab_im + pi[-1] * ab_re)
    pw_re, pw_im = jnp.stack(pr), jnp.stack(pi)
    cr, ci = c_re.astype(F32), c_im.astype(F32)
    eye = jnp.eye(S5_LANE_GROUPS, dtype=BF16)
    nb, lg = S5_BLOCKS, S5_LANE_GROUPS

    def blocks(a):
        return a.reshape(a.shape[0], nb, lg, *a.shape[2:])

    def block_diag(a, perm, eye_axes):
        a = jnp.transpose(blocks(a), perm).astype(BF16)[..., None, :]
        shape = [1] * a.ndim
        shape[eye_axes[0]], shape[eye_axes[1]] = lg, lg
        return a * eye.reshape(shape)

    def widen(small, row_group, col_of, group_of_col):
        n_small, n_wide = small.shape[-1], col_of.shape[0]
        place = (jnp.arange(n_small, dtype=I32)[:, None] == col_of[None, :]).astype(BF16)
        wide = jnp.dot(small.reshape(-1, n_small).astype(BF16), place, preferred_element_type=F32)
        keep = row_group[:, None] == group_of_col[None, :]
        return jnp.where(keep[None], wide.reshape(nb, -1, n_wide), 0.0).astype(BF16)

    dec_re = jnp.stack([pr[t_steps - 1 - s] for s in range(t_steps)])
    dec_im = jnp.stack([pi[t_steps - 1 - s] for s in range(t_steps)])
    si_re = dec_re[..., None] * bb_re[None] - dec_im[..., None] * bb_im[None]
    si_im = dec_re[..., None] * bb_im[None] + dec_im[..., None] * bb_re[None]
    small = jnp.stack([jnp.transpose(blocks(a), (1, 0, 2, 4, 3)) for a in (si_re, si_im)], axis=4)
    small = small.reshape(nb, t_steps * LANES, 2 * SSM_STATE)
    wide_col = jnp.arange(2 * S5_STATE_W, dtype=I32)
    w_in = widen(small, (jnp.arange(t_steps * LANES, dtype=I32) // SSM_GROUP) % lg,
                 (wide_col // S5_STATE_W) * SSM_STATE + wide_col % SSM_STATE, (wide_col // SSM_STATE) % lg)
    up_re, up_im = pw_re[1:t_steps + 1], pw_im[1:t_steps + 1]
    so_re = cr[None] * up_re[:, :, None, :] - ci[None] * up_im[:, :, None, :]
    so_im = cr[None] * up_im[:, :, None, :] + ci[None] * up_re[:, :, None, :]
    small = jnp.stack([jnp.transpose(blocks(a), (1, 2, 4, 0, 3)) for a in (so_re, -so_im)], axis=1)
    small = small.reshape(nb, 2 * S5_STATE_W, t_steps * SSM_GROUP)
    wide_col = jnp.arange(t_steps * LANES, dtype=I32)
    w_out = widen(small, (jnp.arange(2 * S5_STATE_W, dtype=I32) // SSM_STATE) % lg,
                  (wide_col // LANES) * SSM_GROUP + wide_col % SSM_GROUP, (wide_col // SSM_GROUP) % lg)
    lg_re = cr[None] * pw_re[:t_steps, :, None, :] - ci[None] * pw_im[:t_steps, :, None, :]
    lg_im = cr[None] * pw_im[:t_steps, :, None, :] + ci[None] * pw_re[:t_steps, :, None, :]
    k_lag = (jnp.einsum("jgnp,gpm->jgnm", lg_re, bb_re, precision=HIGHEST)
             - jnp.einsum("jgnp,gpm->jgnm", lg_im, bb_im, precision=HIGHEST))
    w_lag = block_diag(k_lag, (1, 0, 2, 4, 3), (2, 4)).reshape(nb, t_steps, LANES, LANES)
    zero = jnp.zeros((nb, LANES, LANES), BF16)
    pairs = []
    for d in range(t_steps // 2):
        rows = []
        for s_local in range(2):
            lag = [2 * d + t_local - s_local for t_local in range(2)]
            rows.append(jnp.concatenate([w_lag[:, j] if j >= 0 else zero for j in lag], axis=2))
        pairs.append(jnp.concatenate(rows, axis=1))
    w_pairs = jnp.stack(pairs, axis=1)
    lam_t = jnp.concatenate([blocks(pw_re[t_steps][None])[0].reshape(nb, 1, S5_STATE_W),
                             blocks(pw_im[t_steps][None])[0].reshape(nb, 1, S5_STATE_W)], axis=2)
    return w_in, w_out, w_pairs, lam_t


def _s5_kernel(*refs, batch, rows_per_batch):
    t_steps = S5_T
    u_refs = refs[:t_steps]
    w_in_ref, w_out_ref, w_pair_ref, lam_ref, d_ref, y_ref, state_ref, carry_ref = refs[t_steps:]
    n_rows = batch * rows_per_batch
    half = S5_STATE_W

    @pl.when(pl.program_id(1) == 0)
    def _():
        carry_ref[...] = jnp.zeros_like(carry_ref)

    us = [r[...].reshape(n_rows, LANES) for r in u_refs]
    u_all = jnp.concatenate(us, axis=1)
    n_slabs = 2 * half // LANES
    contrib = _dot(u_all, w_in_ref[0])
    pitch = rows_per_batch + SUBLANES
    for k in range(n_slabs):
        for b in range(batch):
            state_ref[k, b * pitch:b * pitch + rows_per_batch, :] = (
                contrib[b * rows_per_batch:(b + 1) * rows_per_batch, k * LANES:(k + 1) * LANES])

    lam = lam_ref[0]
    a_re = jnp.broadcast_to(lam[:, :half], (batch, half))
    a_im = jnp.broadcast_to(lam[:, half:], (batch, half))

    def step(c, h):
        rows = pl.ds(c, batch, stride=pitch)
        x = jnp.concatenate([state_ref[k, rows, :] for k in range(n_slabs)], axis=1)
        for k in range(n_slabs):
            state_ref[k, rows, :] = h[:, k * LANES:(k + 1) * LANES]
        h_re, h_im = h[:, :half], h[:, half:]
        n_re = a_re * h_re - a_im * h_im + x[:, :half]
        n_im = a_re * h_im + a_im * h_re + x[:, half:]
        return jnp.concatenate([n_re, n_im], axis=1)

    carry_ref[...] = lax.fori_loop(0, rows_per_batch, step, carry_ref[...])

    entering = jnp.concatenate(
        [jnp.concatenate([state_ref[k, b * pitch:b * pitch + rows_per_batch, :] for b in range(batch)], axis=0)
         for k in range(n_slabs)], axis=1).astype(BF16)
    y_state = _dot(entering, w_out_ref[0])
    d_skip = d_ref[...]
    for a in range(t_steps // 2):
        acc = y_state[:, 2 * a * LANES:(2 * a + 2) * LANES]
        for b in range(a + 1):
            acc = acc + _dot(u_all[:, 2 * b * LANES:(2 * b + 2) * LANES], w_pair_ref[0, a - b])
        for t_local in range(2):
            t = 2 * a + t_local
            y = acc[:, t_local * LANES:(t_local + 1) * LANES] + d_skip * us[t].astype(F32)
            y_ref[:, pl.ds(t, rows_per_batch, stride=t_steps), :] = (
                jax.nn.gelu(y).reshape(batch, rows_per_batch, LANES))


def s5_scan(u, w_in, w_out, w_pairs, lam_t, d_skip, batch, seq):
    t_steps = S5_T
    n_chunk_rows = seq // t_steps
    rows_per_batch = min(128, n_chunk_rows)
    u3 = u.reshape(batch, n_chunk_rows, t_steps * D_MODEL)
    blk = (batch, rows_per_batch, LANES)

    def u_spec(s):
        return pl.BlockSpec(blk, lambda j, c, s=s: (0, c, s * S5_BLOCKS + j))

    return pl.pallas_call(
        functools.partial(_s5_kernel, batch=batch, rows_per_batch=rows_per_batch),
        out_shape=jax.ShapeDtypeStruct((batch, seq, D_MODEL), F32),
        grid=(S5_BLOCKS, n_chunk_rows // rows_per_batch),
        in_specs=[u_spec(s) for s in range(t_steps)] + [
            pl.BlockSpec((1, t_steps * LANES, 2 * S5_STATE_W), lambda j, c: (j, 0, 0)),
            pl.BlockSpec((1, 2 * S5_STATE_W, t_steps * LANES), lambda j, c: (j, 0, 0)),
            pl.BlockSpec((1, t_steps // 2, 2 * LANES, 2 * LANES), lambda j, c: (j, 0, 0, 0)),
            pl.BlockSpec((1, 1, 2 * S5_STATE_W), lambda j, c: (j, 0, 0)),
            pl.BlockSpec((1, LANES), lambda j, c: (0, j)),
        ],
        out_specs=pl.BlockSpec((batch, rows_per_batch * t_steps, LANES), lambda j, c: (0, c, j)),
        scratch_shapes=[
            pltpu.VMEM((2 * S5_STATE_W // LANES, batch * (rows_per_batch + SUBLANES), LANES), F32),
            pltpu.VMEM((batch, 2 * S5_STATE_W), F32),
        ],
        compiler_params=_params(2, VMEM_LIMIT),
        name="s5_scan",
    )(*([u3] * t_steps), w_in, w_out, w_pairs, lam_t, d_skip.reshape(1, D_MODEL))


def _s5_out_kernel(y_ref, x_ref, w_glu_ref, w_out_ref, o_ref):
    y = y_ref[...]
    gate = jax.nn.sigmoid(_dot(y.astype(BF16), w_glu_ref[...]))
    z = (y * gate).astype(BF16)
    o_ref[...] = x_ref[...] + _dot(z, w_out_ref[...])


def s5_out(y, x, w_glu, w_out, rows=512):
    n, d = x.shape
    return pl.pallas_call(
        _s5_out_kernel,
        out_shape=jax.ShapeDtypeStruct((n, d), F32),
        grid=(n // rows,),
        in_specs=[
            pl.BlockSpec((rows, d), lambda i: (i, 0)),
            pl.BlockSpec((rows, d), lambda i: (i, 0)),
            pl.BlockSpec((d, d), lambda i: (0, 0)),
            pl.BlockSpec((d, d), lambda i: (0, 0)),
        ],
        out_specs=pl.BlockSpec((rows, d), lambda i: (i, 0)),
        compiler_params=_params(1),
        name="s5_out",
    )(y, x, w_glu, w_out)


def _ffn_kernel(x_ref, g_ref, wg_ref, wu_ref, wd_ref, o_ref):
    x = x_ref[...]
    xn = _rms(x, g_ref[...]).astype(BF16)
    mid = (jax.nn.silu(_dot(xn, wg_ref[...])) * _dot(xn, wu_ref[...])).astype(BF16)
    o_ref[...] = x + _dot(mid, wd_ref[...])


def dense_ffn(x, g, wg, wu, wd, rows=512):
    n, d = x.shape

    def resident(a):
        return pl.BlockSpec(a.shape, lambda i: (0, 0), pipeline_mode=pl.Buffered(1))

    return pl.pallas_call(
        _ffn_kernel,
        out_shape=jax.ShapeDtypeStruct((n, d), F32),
        grid=(n // rows,),
        in_specs=[
            pl.BlockSpec((rows, d), lambda i: (i, 0)),
            pl.BlockSpec((1, d), lambda i: (0, 0)),
            resident(wg), resident(wu), resident(wd),
        ],
        out_specs=pl.BlockSpec((rows, d), lambda i: (i, 0)),
        compiler_params=_params(1, VMEM_LIMIT),
        name="dense_ffn",
    )(x, g.reshape(1, d), wg, wu, wd)


def _moe_kernel(te_ref, nt_ref, xs_ref, wg_ref, wu_ref, wd_ref, o_ref, *, rows):
    t = pl.program_id(0)

    @pl.when(t < nt_ref[0])
    def _():
        xn = _from_token_tiles(xs_ref, rows).astype(BF16)
        mid = (jax.nn.silu(_dot(xn, wg_ref[0])) * _dot(xn, wu_ref[0])).astype(BF16)
        _to_token_tiles(o_ref, _dot(mid, wd_ref[0]))

    @pl.when(t >= nt_ref[0])
    def _():
        o_ref[...] = jnp.zeros_like(o_ref)


def moe_experts(xs, tile_expert, n_tiles, wg, wu, wd, n_grid_tiles, rows):
    def resident(w):
        return pl.BlockSpec((1,) + w.shape[1:], lambda t, te, nt: (te[t], 0, 0), pipeline_mode=pl.Buffered(1))

    tokens = pl.BlockSpec((rows * TOKEN_TILE, LANES), lambda t, te, nt: (t, 0))

    grid_spec = pltpu.PrefetchScalarGridSpec(
        num_scalar_prefetch=2,
        grid=(n_grid_tiles,),
        in_specs=[tokens, resident(wg), resident(wu), resident(wd)],
        out_specs=tokens,
    )
    return pl.pallas_call(
        functools.partial(_moe_kernel, rows=rows),
        out_shape=jax.ShapeDtypeStruct((n_grid_tiles * rows * TOKEN_TILE, LANES), F32),
        grid_spec=grid_spec,
        compiler_params=_params(1, VMEM_LIMIT),
        name="moe_experts",
    )(tile_expert, n_tiles, xs, wg, wu, wd)


def _rope_kernel(pos_ref, freq_ref, cos_ref, sin_ref):
    ang = freq_ref[...] * pos_ref[...]
    cos, sin = jnp.cos(ang), jnp.sin(ang)
    n = ang.shape[1]
    pad = HEAD_PAD - QK_NOPE - QK_ROPE
    cos_ref[...] = jnp.concatenate([jnp.ones((QK_NOPE, n), F32), cos, cos, jnp.zeros((pad, n), F32)], axis=0)
    sin_ref[...] = jnp.concatenate([jnp.zeros((QK_NOPE, n), F32), sin, sin, jnp.zeros((pad, n), F32)], axis=0)


def rope_tables(positions):
    n = positions.shape[0]
    half = QK_ROPE // 2
    inv_freq = ROPE_BASE ** (-jnp.arange(half, dtype=F32) * (2.0 / QK_ROPE))
    tile = min(n, 4096)
    return pl.pallas_call(
        _rope_kernel,
        out_shape=[jax.ShapeDtypeStruct((HEAD_PAD, n), F32)] * 2,
        grid=(n // tile,),
        in_specs=[pl.BlockSpec((1, tile), lambda i: (0, i)), pl.BlockSpec((half, 1), lambda i: (0, 0))],
        out_specs=[pl.BlockSpec((HEAD_PAD, tile), lambda i: (0, i))] * 2,
        compiler_params=_params(1),
        name="rope_tables",
    )(positions.astype(F32).reshape(1, n), inv_freq.reshape(half, 1))


def _mla_weights(w_dkv, w_ukv, w_uq):
    half = QK_ROPE // 2
    pad = HEAD_PAD - QK_NOPE - QK_ROPE
    scale = (QK_NOPE + QK_ROPE) ** -0.5
    w_dkv_p = jnp.pad(w_dkv, ((0, 0), (0, LANES - QK_ROPE)))
    ukv = w_ukv.reshape(KV_LORA, N_HEADS, QK_NOPE + V_HEAD)
    k_nope = jnp.pad(ukv[:, :, :QK_NOPE], ((0, 0), (0, 0), (0, HEAD_PAD - QK_NOPE)))
    eye = jnp.eye(half, dtype=F32)
    zer = jnp.zeros((half, half), F32)

    def place(x1_to, x2_to):
        blk = jnp.concatenate([jnp.concatenate([x1_to[0], x1_to[1]], axis=1),
                               jnp.concatenate([x2_to[0], x2_to[1]], axis=1)], axis=0)
        return jnp.pad(blk, ((0, 0), (QK_NOPE, pad)))

    rope_a = place((eye, zer), (zer, eye))
    rope_b = place((zer, eye), (-eye, zer))
    rope_a = jnp.tile(rope_a[:, None, :], (1, N_HEADS, 1))
    rope_b = jnp.tile(rope_b[:, None, :], (1, N_HEADS, 1))
    zrows = jnp.zeros((LANES - QK_ROPE, N_HEADS, HEAD_PAD), F32)
    w_ka = jnp.concatenate([k_nope, rope_a, zrows], axis=0).reshape(KV_LORA + LANES, N_HEADS * HEAD_PAD)
    w_kb = jnp.concatenate([rope_b, zrows], axis=0).reshape(LANES, N_HEADS * HEAD_PAD)
    vh = ukv[:, :, QK_NOPE:].reshape(KV_LORA, N_HEADS // 2, 2, V_HEAD)
    w_v = jnp.concatenate([jnp.pad(vh[:, :, 0], ((0, 0), (0, 0), (0, HEAD_PAD - V_HEAD))),
                           jnp.pad(vh[:, :, 1], ((0, 0), (0, 0), (HEAD_PAD - V_HEAD, 0)))], axis=2)
    w_v = w_v.reshape(KV_LORA, N_HEADS * HEAD_PAD)
    uq = w_uq.reshape(Q_LORA, N_HEADS, QK_NOPE + QK_ROPE) * (scale * LOG2_E)
    q_x1, q_x2 = uq[:, :, QK_NOPE:QK_NOPE + half], uq[:, :, QK_NOPE + half:]
    w_qa = jnp.pad(uq, ((0, 0), (0, 0), (0, pad))).reshape(Q_LORA, N_HEADS * HEAD_PAD)
    w_qb = jnp.pad(jnp.concatenate([-q_x2, q_x1], axis=2), ((0, 0), (0, 0), (QK_NOPE, pad)))
    w_qb = w_qb.reshape(Q_LORA, N_HEADS * HEAD_PAD)
    return tuple(a.astype(BF16) for a in (w_dkv_p, w_ka, w_kb, w_v, w_qa, w_qb))


def _qkv_kernel(x_ref, gkv_ref, gq_ref, wdkv_ref, glat_ref, wka_ref, wkb_ref, wv_ref, vone_ref, wdq_ref, gql_ref,
                wqa_ref, wqb_ref, cos_ref, sin_ref, q_ref, k_ref, v_ref):
    x = x_ref[...]
    xn = x * lax.rsqrt(jnp.mean(x * x, axis=-1, keepdims=True) + EPS)
    cos = jnp.tile(cos_ref[...].T, (1, N_HEADS))
    sin = jnp.tile(sin_ref[...].T, (1, N_HEADS))
    ckr = _dot((xn * gkv_ref[...]).astype(BF16), wdkv_ref[...])
    latent = _rms(ckr[:, :KV_LORA], glat_ref[...]).astype(BF16)
    rope_raw = ckr[:, KV_LORA:].astype(BF16)
    ka = _dot(jnp.concatenate([latent, rope_raw], axis=1), wka_ref[...])
    kb = _dot(rope_raw, wkb_ref[...])
    k_ref[...] = (ka * cos + kb * sin).astype(BF16)
    v_ref[...] = (_dot(latent, wv_ref[...]) + vone_ref[...]).astype(BF16)
    cq = _rms(_dot((xn * gq_ref[...]).astype(BF16), wdq_ref[...]), gql_ref[...]).astype(BF16)
    qa = _dot(cq, wqa_ref[...])
    qb = _dot(cq, wqb_ref[...])
    q_ref[...] = (qa * cos + qb * sin).astype(BF16)


def qkv_project(x, g_kv, g_q, g_lat, g_qlat, w_dq, mla_w, cos_t, sin_t, rows=512):
    n, d = x.shape
    w_dkv_p, w_ka, w_kb, w_v, w_qa, w_qb = mla_w
    hw = N_HEADS * HEAD_PAD
    lane = jnp.arange(hw, dtype=I32) % (2 * HEAD_PAD)
    v_one = ((lane == ONE_LANE_EVEN) | (lane == HEAD_PAD + ONE_LANE_ODD)).astype(F32).reshape(1, hw)

    def full(a):
        return pl.BlockSpec(a.shape, lambda i: (0,) * a.ndim, pipeline_mode=pl.Buffered(1))

    def row(width):
        return pl.BlockSpec((rows, width), lambda i: (i, 0))

    args = [x, g_kv.reshape(1, d), g_q.reshape(1, d), w_dkv_p, g_lat.reshape(1, KV_LORA), w_ka, w_kb, w_v, v_one,
            w_dq, g_qlat.reshape(1, Q_LORA), w_qa, w_qb, cos_t, sin_t]
    table = pl.BlockSpec((HEAD_PAD, rows), lambda i: (0, i))
    in_specs = [row(d)] + [full(a) for a in args[1:13]] + [table, table]
    return pl.pallas_call(
        _qkv_kernel,
        out_shape=[jax.ShapeDtypeStruct((n, hw), BF16)] * 3,
        grid=(n // rows,),
        in_specs=in_specs,
        out_specs=[row(hw)] * 3,
        compiler_params=_params(1, VMEM_LIMIT),
        name="qkv_project",
    )(*args)


def _attn_kernel(q_ref, k_ref, v_ref, *rest, tq, tk, td, n_cast):
    cast_in, o_ref, cast_out = rest[:n_cast], rest[n_cast], rest[n_cast + 1:2 * n_cast + 1]
    m_ref, acc_ref = rest[2 * n_cast + 1:]
    for src, dst in zip(cast_in, cast_out):
        dst[...] = src[...].astype(dst.dtype)
    qi = pl.program_id(2)
    m_ref[...] = jnp.full_like(m_ref, NEG_BIG)
    acc_ref[...] = jnp.zeros_like(acc_ref)

    def kv_tile(kv0, width, q0, masked):
        rows = pl.ds(pl.multiple_of(kv0, width), width)
        if masked:
            row_chunk = lax.broadcasted_iota(I32, (tq - q0, width), 0) // CHUNK
            col_chunk = lax.broadcasted_iota(I32, (tq - q0, width), 1) // CHUNK
        for h in range(2):
            cols = slice(h * HEAD_PAD, (h + 1) * HEAD_PAD)
            s = lax.dot_general(q_ref[q0:, cols], k_ref[rows, cols], (((1,), (1,)), ((), ())),
                                preferred_element_type=F32)
            if masked:
                s = jnp.where(col_chunk <= row_chunk, s, NEG_BIG)
            m_old = m_ref[h, q0:, :]
            m_new = jnp.maximum(m_old, jnp.max(s, axis=1, keepdims=True))
            p = jnp.exp2(s - jnp.tile(m_new, (1, width // LANES))).astype(BF16)
            acc_ref[h, q0:, :] = jnp.exp2(m_old - m_new) * acc_ref[h, q0:, :] + _dot(p, v_ref[rows, cols])
            m_ref[h, q0:, :] = m_new

    def full_tile(j, carry):
        kv_tile(j * tk, tk, 0, False)
        return carry

    lax.fori_loop(0, qi * (tq // tk), full_tile, 0)
    for d in range(tq // td):
        kv_tile(qi * tq + d * td, td, d * td, True)
    even, odd = acc_ref[0], acc_ref[1]
    even = even / even[:, ONE_LANE_EVEN:ONE_LANE_EVEN + 1]
    odd = odd / odd[:, ONE_LANE_ODD:ONE_LANE_ODD + 1]
    lane = lax.broadcasted_iota(I32, even.shape, 1)
    o_ref[...] = jnp.where(lane < V_HEAD, even, odd).astype(BF16)


def attention(q, k, v, batch, seq, to_bf16=(), tq=1024, tk=1024, td=512):
    tq, tk, td = min(tq, seq), min(tk, seq), min(td, seq)
    n_q = seq // tq
    pair = 2 * HEAD_PAD
    n_pairs = N_HEADS // 2
    n_steps = batch * n_pairs * n_q
    flat = [w.reshape(-1, w.shape[-1]) for w in to_bf16]

    def slice_spec(w):
        return pl.BlockSpec((w.shape[0] // n_steps, w.shape[1]), lambda b, hp, i: ((b * n_pairs + hp) * n_q + i, 0))

    o, *cast = pl.pallas_call(
        functools.partial(_attn_kernel, tq=tq, tk=tk, td=td, n_cast=len(flat)),
        out_shape=[jax.ShapeDtypeStruct((batch * seq, N_HEADS * V_HEAD), BF16)]
        + [jax.ShapeDtypeStruct(w.shape, BF16) for w in flat],
        grid=(batch, n_pairs, n_q),
        in_specs=[
            pl.BlockSpec((tq, pair), lambda b, hp, i: (b * n_q + i, hp)),
            pl.BlockSpec((seq, pair), lambda b, hp, i: (b, hp)),
            pl.BlockSpec((seq, pair), lambda b, hp, i: (b, hp)),
        ] + [slice_spec(w) for w in flat],
        out_specs=[pl.BlockSpec((tq, 2 * V_HEAD), lambda b, hp, i: (b * n_q + i, hp))]
        + [slice_spec(w) for w in flat],
        scratch_shapes=[pltpu.VMEM((2, tq, LANES), F32), pltpu.VMEM((2, tq, HEAD_PAD), F32)],
        compiler_params=_params(3, VMEM_LIMIT),
        name="attention",
    )(q, k, v, *flat)
    return o, [c.reshape(w.shape) for c, w in zip(cast, to_bf16)]


ROUTE_COLS = 8
TOKEN_TILE = SUBLANES


def _to_token_tiles(ref, x):
    rows = x.shape[0]
    for k in range(x.shape[1] // LANES):
        ref[pl.ds(k, rows, stride=TOKEN_TILE), :] = x[:, k * LANES:(k + 1) * LANES]


def _from_token_tiles(ref, rows):
    return jnp.concatenate([ref[pl.ds(k, rows, stride=TOKEN_TILE), :] for k in range(TOKEN_TILE)], axis=1)


def _attn_out_router_kernel(a_ref, x_ref, w_ref, g_ref, rw_ref, tri_ref, x3_ref, h_ref, route_ref, route_t_ref,
                            cnt_ref, carry_ref):
    @pl.when(pl.program_id(0) == 0)
    def _():
        carry_ref[...] = jnp.zeros_like(carry_ref)

    x3 = x_ref[...] + _dot(a_ref[...], w_ref[...])
    x3_ref[...] = x3
    h = _rms(x3, g_ref[...])
    _to_token_tiles(h_ref, h)
    h_hi = h.astype(BF16)
    h_lo = (h - h_hi.astype(F32)).astype(BF16)
    by_hi = _dot(h_hi, rw_ref[...])
    by_lo = _dot(h_lo, rw_ref[...])
    logits = by_hi[:, :N_EXPERTS] + by_hi[:, N_EXPERTS:] + by_lo[:, :N_EXPERTS]
    col = lax.broadcasted_iota(I32, logits.shape, 1)
    m1 = jnp.max(logits, axis=1, keepdims=True)
    i1 = jnp.min(jnp.where(logits == m1, col, N_EXPERTS), axis=1, keepdims=True)
    rest = jnp.where(col == i1, -jnp.inf, logits)
    m2 = jnp.max(rest, axis=1, keepdims=True)
    i2 = jnp.min(jnp.where(rest == m2, col, N_EXPERTS), axis=1, keepdims=True)
    e2 = jnp.exp(m2 - m1)
    w1 = 1.0 / (1.0 + e2)
    w2 = e2 / (1.0 + e2)
    sel = jnp.where((col == i1) | (col == i2), 1.0, 0.0)
    incl = _dot(tri_ref[...], sel.astype(BF16))
    carry = carry_ref[0:1, :]
    excl = incl - sel + carry
    r1 = jnp.sum(jnp.where(col == i1, excl, 0.0), axis=1, keepdims=True)
    r2 = jnp.sum(jnp.where(col == i2, excl, 0.0), axis=1, keepdims=True)
    total = carry + incl[incl.shape[0] - 1:, :]
    carry_ref[...] = jnp.broadcast_to(total, carry_ref.shape)
    cnt_ref[...] = jnp.broadcast_to(total, cnt_ref.shape)
    route = jnp.where(col == 0, i1.astype(F32), 0.0)
    for c, val in ((1, i2.astype(F32)), (2, r1), (3, r2), (4, w1), (5, w2)):
        route = jnp.where(col == c, val, route)
    route_ref[...] = route
    eye = (lax.broadcasted_iota(I32, (ROUTE_COLS, ROUTE_COLS), 0)
           == lax.broadcasted_iota(I32, (ROUTE_COLS, ROUTE_COLS), 1)).astype(F32)
    route_t_ref[...] = lax.dot_general(eye, route, (((1,), (1,)), ((), ())), precision=HIGHEST,
                                       preferred_element_type=F32)


def attn_out_router(a, x, w_o, g, router_w, rows=1024):
    n, d = x.shape
    rows = min(rows, n)
    tri = (jnp.arange(rows)[:, None] >= jnp.arange(rows)[None, :]).astype(BF16)
    rw_hi = router_w.astype(BF16)
    rw_lo = (router_w - rw_hi.astype(F32)).astype(BF16)
    router_w = jnp.concatenate([rw_hi, rw_lo], axis=1)
    return pl.pallas_call(
        _attn_out_router_kernel,
        out_shape=[jax.ShapeDtypeStruct((n, d), F32), jax.ShapeDtypeStruct((n * TOKEN_TILE, LANES), F32),
                   jax.ShapeDtypeStruct((n, ROUTE_COLS), F32), jax.ShapeDtypeStruct((ROUTE_COLS, n), F32),
                   jax.ShapeDtypeStruct((8, N_EXPERTS), F32)],
        grid=(n // rows,),
        in_specs=[
            pl.BlockSpec((rows, a.shape[1]), lambda i: (i, 0)),
            pl.BlockSpec((rows, d), lambda i: (i, 0)),
            pl.BlockSpec(w_o.shape, lambda i: (0, 0)),
            pl.BlockSpec((1, d), lambda i: (0, 0)),
            pl.BlockSpec((d, 2 * N_EXPERTS), lambda i: (0, 0)),
            pl.BlockSpec((rows, rows), lambda i: (0, 0)),
        ],
        out_specs=[
            pl.BlockSpec((rows, d), lambda i: (i, 0)),
            pl.BlockSpec((rows * TOKEN_TILE, LANES), lambda i: (i, 0)),
            pl.BlockSpec((rows, ROUTE_COLS), lambda i: (i, 0)),
            pl.BlockSpec((ROUTE_COLS, rows), lambda i: (0, i)),
            pl.BlockSpec((8, N_EXPERTS), lambda i: (0, 0)),
        ],
        scratch_shapes=[pltpu.VMEM((8, N_EXPERTS), F32)],
        compiler_params=_params(1, VMEM_LIMIT),
        name="attn_out_router",
    )(a, x, w_o, g.reshape(1, d), router_w, tri)


def _tile_copy(src_ref, src_token, dst_ref, dst_token, sem):
    src = src_ref.at[pl.ds(pl.multiple_of(src_token * TOKEN_TILE, TOKEN_TILE), TOKEN_TILE)]
    dst = dst_ref.at[pl.ds(pl.multiple_of(dst_token * TOKEN_TILE, TOKEN_TILE), TOKEN_TILE)]
    return pltpu.make_async_copy(src, dst, sem)


ISSUE_UNROLL = 8


def _dispatch_kernel(zoff_ref, nt_ref, p0_ref, p1_ref, h_ref, xs_ref, zero_ref, sem, zsem, *,
                     rows, tile_rows, n_buf_tiles):
    tile_words = tile_rows * TOKEN_TILE

    @pl.when(pl.program_id(0) == 0)
    def _():
        zero_ref[...] = jnp.zeros_like(zero_ref)
        copies = [pltpu.make_async_copy(
            zero_ref, xs_ref.at[pl.ds(pl.multiple_of(zoff_ref[e] * TOKEN_TILE, TOKEN_TILE), tile_words)], zsem)
            for e in range(N_EXPERTS)]
        for c in copies:
            c.start()
        for c in copies:
            c.wait()
        def fill_idle(t, carry):
            c = pltpu.make_async_copy(
                zero_ref, xs_ref.at[pl.ds(pl.multiple_of(t * tile_words, tile_words), tile_words)], zsem)
            c.start()
            c.wait()
            return carry

        lax.fori_loop(nt_ref[0], n_buf_tiles, fill_idle, 0)

    def issue(r, carry):
        _tile_copy(h_ref, r, xs_ref, p0_ref[r], sem).start(priority=0)
        _tile_copy(h_ref, r, xs_ref, p1_ref[r], sem).start(priority=1)
        return carry

    lax.fori_loop(0, rows, issue, 0, unroll=ISSUE_UNROLL)
    for _ in range(2):
        pltpu.make_async_copy(h_ref, xs_ref.at[pl.ds(0, rows * TOKEN_TILE)], sem).wait()


def dispatch(h, pos0, pos1, zero_off, n_tiles, n_buf_tiles, tile_rows, rows=1024):
    n = h.shape[0] // TOKEN_TILE
    rows = min(rows, n)
    n_steps = n // rows
    grid_spec = pltpu.PrefetchScalarGridSpec(
        num_scalar_prefetch=2,
        grid=(n_steps,),
        in_specs=[
            pl.BlockSpec((rows,), lambda i, z, t: (i,), memory_space=pltpu.SMEM),
            pl.BlockSpec((rows,), lambda i, z, t: (i,), memory_space=pltpu.SMEM),
            pl.BlockSpec((rows * TOKEN_TILE, LANES), lambda i, z, t: (i, 0)),
        ],
        out_specs=pl.BlockSpec(memory_space=pl.ANY),
        scratch_shapes=[pltpu.VMEM((tile_rows * TOKEN_TILE, LANES), F32), pltpu.SemaphoreType.DMA,
                        pltpu.SemaphoreType.DMA],
    )
    return pl.pallas_call(
        functools.partial(_dispatch_kernel, rows=rows, tile_rows=tile_rows, n_buf_tiles=n_buf_tiles),
        out_shape=jax.ShapeDtypeStruct((n_buf_tiles * tile_rows * TOKEN_TILE, LANES), F32),
        grid_spec=grid_spec,
        compiler_params=_params(1, VMEM_LIMIT),
        name="dispatch",
    )(zero_off, n_tiles, pos0, pos1, h)


def _combine_kernel(p0_ref, p1_ref, q0_ref, q1_ref, ys_ref, x_ref, w_ref, g_ref, o_ref, buf, sem, *, rows):
    i = pl.program_id(0)
    slot = i % 2

    def issue(a_ref, b_ref, to_slot):
        def body(r, carry):
            _tile_copy(ys_ref, a_ref[r], buf.at[to_slot, 0], r, sem.at[to_slot]).start(priority=0)
            _tile_copy(ys_ref, b_ref[r], buf.at[to_slot, 1], r, sem.at[to_slot]).start(priority=1)
            return carry

        lax.fori_loop(0, rows, body, 0, unroll=ISSUE_UNROLL)

    @pl.when(i == 0)
    def _():
        issue(p0_ref, p1_ref, 0)

    @pl.when(i + 1 < pl.num_programs(0))
    def _():
        issue(q0_ref, q1_ref, 1 - slot)

    for k in range(2):
        pltpu.make_async_copy(ys_ref.at[pl.ds(0, rows * TOKEN_TILE)], buf.at[slot, k], sem.at[slot]).wait()
    w = w_ref[...]
    y = (x_ref[...] + w[:, 4:5] * _from_token_tiles(buf.at[slot, 0], rows)
         + w[:, 5:6] * _from_token_tiles(buf.at[slot, 1], rows))
    o_ref[...] = _rms(y, g_ref[...])


def combine(ys, pos0, pos1, x, route, g, rows=512):
    n, d = x.shape
    rows = min(rows, n)
    n_blocks = n // rows

    def cur(i):
        return (i,)

    def nxt(i):
        return (jnp.minimum(i + 1, n_blocks - 1),)

    return pl.pallas_call(
        functools.partial(_combine_kernel, rows=rows),
        out_shape=jax.ShapeDtypeStruct((n, d), F32),
        grid=(n_blocks,),
        in_specs=[
            pl.BlockSpec((rows,), cur, memory_space=pltpu.SMEM),
            pl.BlockSpec((rows,), cur, memory_space=pltpu.SMEM),
            pl.BlockSpec((rows,), nxt, memory_space=pltpu.SMEM),
            pl.BlockSpec((rows,), nxt, memory_space=pltpu.SMEM),
            pl.BlockSpec(memory_space=pl.ANY),
            pl.BlockSpec((rows, d), lambda i: (i, 0)),
            pl.BlockSpec((rows, ROUTE_COLS), lambda i: (i, 0)),
            pl.BlockSpec((1, d), lambda i: (0, 0)),
        ],
        out_specs=pl.BlockSpec((rows, d), lambda i: (i, 0)),
        scratch_shapes=[pltpu.VMEM((2, 2, rows * TOKEN_TILE, LANES), F32), pltpu.SemaphoreType.DMA((2,))],
        compiler_params=_params(1, VMEM_LIMIT),
        name="combine",
    )(pos0, pos1, pos0, pos1, ys, x, route, g.reshape(1, d))


def moe_layer(attn, x2, w_o, g_ffn, g_final, router_w, wg, wu, wd, tile_rows=512):
    n, d = x2.shape
    tile_rows = min(tile_rows, n)
    x3, h, route, route_t, counts = attn_out_router(attn, x2, w_o, g_ffn, router_w)
    counts = counts[0].astype(I32)
    tiles = (counts + tile_rows - 1) // tile_rows
    tile_start = jnp.cumsum(tiles) - tiles
    row_start = tile_start * tile_rows
    n_tiles = jnp.sum(tiles).reshape(1).astype(I32)
    max_tiles = (2 * n) // tile_rows + N_EXPERTS
    n_buf_tiles = max_tiles + 1
    e0, e1, rank0, rank1 = (route_t[c].astype(I32) for c in range(4))
    experts = jnp.arange(N_EXPERTS, dtype=I32)[:, None]

    def start_of(e):
        return jnp.sum(jnp.where(e[None, :] == experts, row_start[:, None], 0), axis=0)

    pos0 = start_of(e0) + rank0
    pos1 = start_of(e1) + rank1
    t_idx = jnp.arange(max_tiles, dtype=I32)
    tile_expert = jnp.sum(t_idx[:, None] >= (tile_start + tiles)[None, :], axis=1).astype(I32)
    tile_expert = jnp.where(t_idx < n_tiles[0], jnp.minimum(tile_expert, N_EXPERTS - 1),
                            tile_expert[n_tiles[0] - 1])
    zero_off = (row_start + counts).astype(I32)
    xs = dispatch(h, pos0, pos1, zero_off, n_tiles, n_buf_tiles, tile_rows)
    ys = moe_experts(xs, tile_expert, n_tiles, wg, wu, wd, max_tiles, tile_rows)
    return combine(ys, pos0, pos1, x3, route, g_final)


def kernel(x, positions, norm_mix, norm_ffn, final_norm, s5_w_in, s5_lambda_re, s5_lambda_im, s5_log_dt, s5_b_re, s5_b_im, s5_c_re, s5_c_im, s5_d, s5_w_glu, s5_w_out, kv_norm, w_dkv, kv_latent_norm, w_ukv, w_dq, q_latent_norm, w_uq, w_o, ffn_w_gate, ffn_w_up, ffn_w_down, router_w, moe_w_gate, moe_w_up, moe_w_down):
    batch, seq, d = x.shape
    n = batch * seq
    x0 = x.reshape(n, d)
    u = norm_matmul_chunked(x0, norm_mix[0], s5_w_in[0].astype(BF16), BF16, S5_T)
    s5_w = _s5_weights(s5_lambda_re[0], s5_lambda_im[0], s5_log_dt[0], s5_b_re[0], s5_b_im[0],
                       s5_c_re[0], s5_c_im[0])
    y = s5_scan(u, *s5_w, s5_d[0], batch, seq).reshape(n, d)
    x1 = s5_out(y, x0, s5_w_glu[0].astype(BF16), s5_w_out[0].astype(BF16))
    x2 = dense_ffn(x1, norm_ffn[0], ffn_w_gate[0].astype(BF16), ffn_w_up[0].astype(BF16),
                   ffn_w_down[0].astype(BF16))
    cos_tab, sin_tab = rope_tables(positions.reshape(n))
    mla_w = _mla_weights(w_dkv, w_ukv, w_uq[0])
    q, k, v = qkv_project(x2, kv_norm, norm_mix[1], kv_latent_norm, q_latent_norm[0], w_dq[0].astype(BF16),
                          mla_w, cos_tab, sin_tab)
    o, moe_w = attention(q, k, v, batch, seq, to_bf16=(moe_w_gate[0], moe_w_up[0], moe_w_down[0]))
    out = moe_layer(o, x2, w_o[0].astype(BF16), norm_ffn[1], final_norm, router_w[0], *moe_w)
    return out.reshape(batch, seq, d)
```

```python
import functools

import jax
import jax.numpy as jnp
from jax import lax
from jax.experimental import pallas as pl
from jax.experimental.pallas import tpu as pltpu

F32 = jnp.float32
BF16 = jnp.bfloat16
I32 = jnp.int32
HIGHEST = lax.Precision.HIGHEST

D_MODEL = 1024
CHUNK = 64
SSM_GROUP = 16
SSM_GROUPS = D_MODEL // SSM_GROUP
SSM_STATE = 64
N_HEADS = 16
QK_NOPE = 64
QK_ROPE = 32
V_HEAD = 64
Q_LORA = 512
KV_LORA = 256
ROPE_BASE = 10000.0
D_FF = 2688
N_EXPERTS = 8
MOE_FF = 3584
EPS = 1e-6

LANES = 128
SUBLANES = 8
HEAD_PAD = 128
S5_T = 8
S5_LANE_GROUPS = LANES // SSM_GROUP
S5_BLOCKS = D_MODEL // LANES
S5_STATE_W = S5_LANE_GROUPS * SSM_STATE
NEG_BIG = -1e30
LOG2_E = 1.4426950408889634
ONE_LANE_EVEN = V_HEAD
ONE_LANE_ODD = 0
VMEM_LIMIT = 56 * 1024 * 1024
MOE_VMEM_LIMIT = 62 * 1024 * 1024


def _params(n_axes, vmem=None):
    return pltpu.CompilerParams(dimension_semantics=("arbitrary",) * n_axes, vmem_limit_bytes=vmem)


def _rms(x, g):
    return x * lax.rsqrt(jnp.mean(x * x, axis=-1, keepdims=True) + EPS) * g


def _dot(a, b):
    return jnp.dot(a, b, preferred_element_type=F32)


def _norm_matmul_chunked_kernel(x_ref, g_ref, w_ref, o_ref, slab_ref, *, t_steps):
    h = _rms(x_ref[...], g_ref[...]).astype(BF16)
    res = _dot(h, w_ref[...])
    rows, dout = res.shape
    for k in range(dout // LANES):
        slab_ref[k] = res[:, k * LANES:(k + 1) * LANES]
    for s in range(t_steps):
        for k in range(dout // LANES):
            lanes = slice(s * dout + k * LANES, s * dout + (k + 1) * LANES)
            o_ref[:, lanes] = slab_ref[k, pl.ds(s, rows // t_steps, stride=t_steps), :].astype(o_ref.dtype)


def norm_matmul_chunked(x, g, w, out_dtype, t_steps, rows=512):
    n, din = x.shape
    dout = w.shape[1]
    return pl.pallas_call(
        functools.partial(_norm_matmul_chunked_kernel, t_steps=t_steps),
        out_shape=jax.ShapeDtypeStruct((n // t_steps, t_steps * dout), out_dtype),
        grid=(n // rows,),
        in_specs=[
            pl.BlockSpec((rows, din), lambda i: (i, 0)),
            pl.BlockSpec((1, din), lambda i: (0, 0)),
            pl.BlockSpec((din, dout), lambda i: (0, 0)),
        ],
        out_specs=pl.BlockSpec((rows // t_steps, t_steps * dout), lambda i: (i, 0)),
        scratch_shapes=[pltpu.VMEM((dout // LANES, rows, LANES), F32)],
        compiler_params=_params(1),
        name="norm_matmul_chunked",
    )(x, g.reshape(1, din), w)


def _s5_weights(lam_re, lam_im, log_dt, b_re, b_im, c_re, c_im):
    t_steps = S5_T
    lr, li = lam_re.astype(F32), lam_im.astype(F32)
    dt = jnp.exp(log_dt.astype(F32))[:, None]
    mag = jnp.exp(lr * dt)
    ab_re, ab_im = mag * jnp.cos(li * dt), mag * jnp.sin(li * dt)
    den = lr * lr + li * li
    nr, ni = ab_re - 1.0, ab_im
    coef_re = (nr * lr + ni * li) / den
    coef_im = (ni * lr - nr * li) / den
    br, bi = b_re.astype(F32), b_im.astype(F32)
    bb_re = coef_re[..., None] * br - coef_im[..., None] * bi
    bb_im = coef_re[..., None] * bi + coef_im[..., None] * br
    pr, pi = [jnp.ones_like(ab_re)], [jnp.zeros_like(ab_im)]
    for _ in range(t_steps):
        pr.append(pr[-1] * ab_re - pi[-1] * ab_im)
        pi.append(pr[-2] * ab_im + pi[-1] * ab_re)
    pw_re, pw_im = jnp.stack(pr), jnp.stack(pi)
    cr, ci = c_re.astype(F32), c_im.astype(F32)
    nb, lg = S5_BLOCKS, S5_LANE_GROUPS

    def blocks(a):
        return a.reshape(a.shape[0], nb, lg, *a.shape[2:])

    def widen(small, row_group, col_of, group_of_col):
        n_small, n_wide = small.shape[-1], col_of.shape[0]
        place = (jnp.arange(n_small, dtype=I32)[:, None] == col_of[None, :]).astype(BF16)
        wide = jnp.dot(small.reshape(-1, n_small).astype(BF16), place, preferred_element_type=F32)
        keep = row_group[:, None] == group_of_col[None, :]
        return jnp.where(keep[None], wide.reshape(nb, -1, n_wide), 0.0).astype(BF16)

    dec_re = jnp.stack([pr[t_steps - 1 - s] for s in range(t_steps)])
    dec_im = jnp.stack([pi[t_steps - 1 - s] for s in range(t_steps)])
    si_re = dec_re[..., None] * bb_re[None] - dec_im[..., None] * bb_im[None]
    si_im = dec_re[..., None] * bb_im[None] + dec_im[..., None] * bb_re[None]
    small = jnp.stack([jnp.transpose(blocks(a), (1, 0, 2, 4, 3)) for a in (si_re, si_im)], axis=4)
    small = small.reshape(nb, t_steps * LANES, 2 * SSM_STATE)
    wide_col = jnp.arange(2 * S5_STATE_W, dtype=I32)
    w_in = widen(small, (jnp.arange(t_steps * LANES, dtype=I32) // SSM_GROUP) % lg,
                 (wide_col // S5_STATE_W) * SSM_STATE + wide_col % SSM_STATE, (wide_col // SSM_STATE) % lg)
    up_re, up_im = pw_re[1:t_steps + 1], pw_im[1:t_steps + 1]
    so_re = cr[None] * up_re[:, :, None, :] - ci[None] * up_im[:, :, None, :]
    so_im = cr[None] * up_im[:, :, None, :] + ci[None] * up_re[:, :, None, :]
    small = jnp.stack([jnp.transpose(blocks(a), (1, 2, 4, 0, 3)) for a in (so_re, -so_im)], axis=1)
    small = small.reshape(nb, 2 * S5_STATE_W, t_steps * SSM_GROUP)
    wide_col = jnp.arange(t_steps * LANES, dtype=I32)
    w_out = widen(small, (jnp.arange(2 * S5_STATE_W, dtype=I32) // SSM_STATE) % lg,
                  (wide_col // LANES) * SSM_GROUP + wide_col % SSM_GROUP, (wide_col // SSM_GROUP) % lg)
    lg_re = cr[None] * pw_re[:t_steps, :, None, :] - ci[None] * pw_im[:t_steps, :, None, :]
    lg_im = cr[None] * pw_im[:t_steps, :, None, :] + ci[None] * pw_re[:t_steps, :, None, :]
    bt_re, bt_im = jnp.swapaxes(bb_re, 1, 2), jnp.swapaxes(bb_im, 1, 2)
    k_lag = jnp.sum(lg_re[:, :, None, :, :] * bt_re[None, :, :, None, :]
                    - lg_im[:, :, None, :, :] * bt_im[None, :, :, None, :], axis=-1)
    small = jnp.transpose(blocks(k_lag), (1, 0, 2, 3, 4)).reshape(nb, t_steps * LANES, SSM_GROUP)
    wide_col = jnp.arange(LANES, dtype=I32)
    w_lag = widen(small, (jnp.arange(t_steps * LANES, dtype=I32) // SSM_GROUP) % lg,
                  wide_col % SSM_GROUP, wide_col // SSM_GROUP).reshape(nb, t_steps, LANES, LANES)
    zero = jnp.zeros((nb, LANES, LANES), BF16)
    pairs = []
    for d in range(t_steps // 2):
        rows = []
        for s_local in range(2):
            lag = [2 * d + t_local - s_local for t_local in range(2)]
            rows.append(jnp.concatenate([w_lag[:, j] if j >= 0 else zero for j in lag], axis=2))
        pairs.append(jnp.concatenate(rows, axis=1))
    w_pairs = jnp.stack(pairs, axis=1)
    lam_t = jnp.concatenate([blocks(pw_re[t_steps][None])[0].reshape(nb, 1, S5_STATE_W),
                             blocks(pw_im[t_steps][None])[0].reshape(nb, 1, S5_STATE_W)], axis=2)
    return w_in, w_out, w_pairs, lam_t


def _s5_kernel(*refs, batch, rows_per_batch):
    t_steps = S5_T
    u_refs = refs[:t_steps]
    w_in_ref, w_out_ref, w_pair_ref, lam_ref, d_ref, y_ref, state_ref, carry_ref = refs[t_steps:]
    n_rows = batch * rows_per_batch
    half = S5_STATE_W

    @pl.when(pl.program_id(1) == 0)
    def _():
        carry_ref[...] = jnp.zeros_like(carry_ref)

    us = [r[...].reshape(n_rows, LANES) for r in u_refs]
    u_all = jnp.concatenate(us, axis=1)
    n_slabs = 2 * half // LANES
    contrib = _dot(u_all, w_in_ref[0])
    pitch = rows_per_batch + SUBLANES
    for k in range(n_slabs):
        for b in range(batch):
            state_ref[k, b * pitch:b * pitch + rows_per_batch, :] = (
                contrib[b * rows_per_batch:(b + 1) * rows_per_batch, k * LANES:(k + 1) * LANES])

    lam = lam_ref[0]
    a_re = jnp.broadcast_to(lam[:, :half], (batch, half))
    a_im = jnp.broadcast_to(lam[:, half:], (batch, half))

    def step(c, h):
        rows = pl.ds(c, batch, stride=pitch)
        x = jnp.concatenate([state_ref[k, rows, :] for k in range(n_slabs)], axis=1)
        for k in range(n_slabs):
            state_ref[k, rows, :] = h[:, k * LANES:(k + 1) * LANES]
        h_re, h_im = h[:, :half], h[:, half:]
        n_re = a_re * h_re - a_im * h_im + x[:, :half]
        n_im = a_re * h_im + a_im * h_re + x[:, half:]
        return jnp.concatenate([n_re, n_im], axis=1)

    carry_ref[...] = lax.fori_loop(0, rows_per_batch, step, carry_ref[...])

    entering = jnp.concatenate(
        [jnp.concatenate([state_ref[k, b * pitch:b * pitch + rows_per_batch, :] for b in range(batch)], axis=0)
         for k in range(n_slabs)], axis=1).astype(BF16)
    y_state = _dot(entering, w_out_ref[0])
    d_skip = d_ref[...]
    for a in range(t_steps // 2):
        acc = y_state[:, 2 * a * LANES:(2 * a + 2) * LANES]
        for b in range(a + 1):
            acc = acc + _dot(u_all[:, 2 * b * LANES:(2 * b + 2) * LANES], w_pair_ref[0, a - b])
        for t_local in range(2):
            t = 2 * a + t_local
            y = acc[:, t_local * LANES:(t_local + 1) * LANES] + d_skip * us[t].astype(F32)
            y_ref[:, pl.ds(t, rows_per_batch, stride=t_steps), :] = (
                jax.nn.gelu(y).reshape(batch, rows_per_batch, LANES))


def s5_scan(u, w_in, w_out, w_pairs, lam_t, d_skip, batch, seq):
    t_steps = S5_T
    n_chunk_rows = seq // t_steps
    rows_per_batch = min(128, n_chunk_rows)
    u3 = u.reshape(batch, n_chunk_rows, t_steps * D_MODEL)
    blk = (batch, rows_per_batch, LANES)

    def u_spec(s):
        return pl.BlockSpec(blk, lambda j, c, s=s: (0, c, s * S5_BLOCKS + j))

    return pl.pallas_call(
        functools.partial(_s5_kernel, batch=batch, rows_per_batch=rows_per_batch),
        out_shape=jax.ShapeDtypeStruct((batch, seq, D_MODEL), F32),
        grid=(S5_BLOCKS, n_chunk_rows // rows_per_batch),
        in_specs=[u_spec(s) for s in range(t_steps)] + [
            pl.BlockSpec((1, t_steps * LANES, 2 * S5_STATE_W), lambda j, c: (j, 0, 0)),
            pl.BlockSpec((1, 2 * S5_STATE_W, t_steps * LANES), lambda j, c: (j, 0, 0)),
            pl.BlockSpec((1, t_steps // 2, 2 * LANES, 2 * LANES), lambda j, c: (j, 0, 0, 0)),
            pl.BlockSpec((1, 1, 2 * S5_STATE_W), lambda j, c: (j, 0, 0)),
            pl.BlockSpec((1, LANES), lambda j, c: (0, j)),
        ],
        out_specs=pl.BlockSpec((batch, rows_per_batch * t_steps, LANES), lambda j, c: (0, c, j)),
        scratch_shapes=[
            pltpu.VMEM((2 * S5_STATE_W // LANES, batch * (rows_per_batch + SUBLANES), LANES), F32),
            pltpu.VMEM((batch, 2 * S5_STATE_W), F32),
        ],
        compiler_params=_params(2, VMEM_LIMIT),
        name="s5_scan",
    )(*([u3] * t_steps), w_in, w_out, w_pairs, lam_t, d_skip.reshape(1, D_MODEL))


def _s5_out_ffn_kernel(y_ref, x_ref, w_glu_ref, w_out_ref, g_ref, wg_ref, wu_ref, wd_ref, o_ref):
    y = y_ref[...]
    gate = jax.nn.sigmoid(_dot(y.astype(BF16), w_glu_ref[...]))
    x1 = x_ref[...] + _dot((y * gate).astype(BF16), w_out_ref[...])
    xn = _rms(x1, g_ref[...]).astype(BF16)
    mid = (jax.nn.silu(_dot(xn, wg_ref[...])) * _dot(xn, wu_ref[...])).astype(BF16)
    o_ref[...] = x1 + _dot(mid, wd_ref[...])


def s5_out_ffn(y, x, w_glu, w_out, g, wg, wu, wd, rows=512):
    n, d = x.shape

    def resident(a):
        return pl.BlockSpec(a.shape, lambda i: (0, 0), pipeline_mode=pl.Buffered(1))

    def row():
        return pl.BlockSpec((rows, d), lambda i: (i, 0))

    g = g.reshape(1, d)
    return pl.pallas_call(
        _s5_out_ffn_kernel,
        out_shape=jax.ShapeDtypeStruct((n, d), F32),
        grid=(n // rows,),
        in_specs=[row(), row()] + [resident(a) for a in (w_glu, w_out, g, wg, wu, wd)],
        out_specs=row(),
        compiler_params=_params(1, VMEM_LIMIT),
        name="s5_out_ffn",
    )(y, x, w_glu, w_out, g, wg, wu, wd)


def _moe_kernel(te_ref, nt_ref, xs_ref, wg_ref, wu_ref, wd_ref, o_ref, *, rows):
    t = pl.program_id(0)

    @pl.when(t < nt_ref[0])
    def _():
        xn = _from_token_tiles(xs_ref, rows).astype(BF16)
        mid = (jax.nn.silu(_dot(xn, wg_ref[0])) * _dot(xn, wu_ref[0])).astype(BF16)
        _to_token_tiles(o_ref, _dot(mid, wd_ref[0]))

    @pl.when(t >= nt_ref[0])
    def _():
        o_ref[...] = jnp.zeros_like(o_ref)


def moe_experts(xs, tile_expert, n_tiles, wg, wu, wd, n_grid_tiles, rows):
    def resident(w):
        return pl.BlockSpec((1,) + w.shape[1:], lambda t, te, nt: (te[t], 0, 0), pipeline_mode=pl.Buffered(2))

    tokens = pl.BlockSpec((rows * TOKEN_TILE, LANES), lambda t, te, nt: (t, 0))

    grid_spec = pltpu.PrefetchScalarGridSpec(
        num_scalar_prefetch=2,
        grid=(n_grid_tiles,),
        in_specs=[tokens, resident(wg), resident(wu), resident(wd)],
        out_specs=tokens,
    )
    return pl.pallas_call(
        functools.partial(_moe_kernel, rows=rows),
        out_shape=jax.ShapeDtypeStruct((n_grid_tiles * rows * TOKEN_TILE, LANES), F32),
        grid_spec=grid_spec,
        compiler_params=_params(1, MOE_VMEM_LIMIT),
        name="moe_experts",
    )(tile_expert, n_tiles, xs, wg, wu, wd)


def _rope_kernel(pos_ref, freq_ref, cos_ref, sin_ref):
    ang = freq_ref[...] * pos_ref[...]
    cos, sin = jnp.cos(ang), jnp.sin(ang)
    n = ang.shape[1]
    pad = HEAD_PAD - QK_NOPE - QK_ROPE
    cos_ref[...] = jnp.concatenate([jnp.ones((QK_NOPE, n), F32), cos, cos, jnp.zeros((pad, n), F32)], axis=0)
    sin_ref[...] = jnp.concatenate([jnp.zeros((QK_NOPE, n), F32), sin, sin, jnp.zeros((pad, n), F32)], axis=0)


def rope_tables(positions):
    n = positions.shape[0]
    half = QK_ROPE // 2
    inv_freq = ROPE_BASE ** (-jnp.arange(half, dtype=F32) * (2.0 / QK_ROPE))
    tile = min(n, 4096)
    return pl.pallas_call(
        _rope_kernel,
        out_shape=[jax.ShapeDtypeStruct((HEAD_PAD, n), F32)] * 2,
        grid=(n // tile,),
        in_specs=[pl.BlockSpec((1, tile), lambda i: (0, i)), pl.BlockSpec((half, 1), lambda i: (0, 0))],
        out_specs=[pl.BlockSpec((HEAD_PAD, tile), lambda i: (0, i))] * 2,
        compiler_params=_params(1),
        name="rope_tables",
    )(positions.astype(F32).reshape(1, n), inv_freq.reshape(half, 1))


def _mla_weights(w_dkv, w_ukv, w_uq):
    half = QK_ROPE // 2
    pad = HEAD_PAD - QK_NOPE - QK_ROPE
    scale = (QK_NOPE + QK_ROPE) ** -0.5
    w_dkv_p = jnp.pad(w_dkv, ((0, 0), (0, LANES - QK_ROPE)))
    ukv = w_ukv.reshape(KV_LORA, N_HEADS, QK_NOPE + V_HEAD)
    k_nope = jnp.pad(ukv[:, :, :QK_NOPE], ((0, 0), (0, 0), (0, HEAD_PAD - QK_NOPE)))
    eye = jnp.eye(half, dtype=F32)
    zer = jnp.zeros((half, half), F32)

    def place(x1_to, x2_to):
        blk = jnp.concatenate([jnp.concatenate([x1_to[0], x1_to[1]], axis=1),
                               jnp.concatenate([x2_to[0], x2_to[1]], axis=1)], axis=0)
        return jnp.pad(blk, ((0, 0), (QK_NOPE, pad)))

    rope_a = place((eye, zer), (zer, eye))
    rope_b = place((zer, eye), (-eye, zer))
    rope_a = jnp.tile(rope_a[:, None, :], (1, N_HEADS, 1))
    rope_b = jnp.tile(rope_b[:, None, :], (1, N_HEADS, 1))
    zrows = jnp.zeros((LANES - QK_ROPE, N_HEADS, HEAD_PAD), F32)
    w_ka = jnp.concatenate([k_nope, rope_a, zrows], axis=0).reshape(KV_LORA + LANES, N_HEADS * HEAD_PAD)
    w_kb = jnp.concatenate([rope_b, zrows], axis=0).reshape(LANES, N_HEADS * HEAD_PAD)
    vh = ukv[:, :, QK_NOPE:].reshape(KV_LORA, N_HEADS // 2, 2, V_HEAD)
    w_v = jnp.concatenate([jnp.pad(vh[:, :, 0], ((0, 0), (0, 0), (0, HEAD_PAD - V_HEAD))),
                           jnp.pad(vh[:, :, 1], ((0, 0), (0, 0), (HEAD_PAD - V_HEAD, 0)))], axis=2)
    w_v = w_v.reshape(KV_LORA, N_HEADS * HEAD_PAD)
    uq = w_uq.reshape(Q_LORA, N_HEADS, QK_NOPE + QK_ROPE) * (scale * LOG2_E)
    q_x1, q_x2 = uq[:, :, QK_NOPE:QK_NOPE + half], uq[:, :, QK_NOPE + half:]
    w_qa = jnp.pad(uq, ((0, 0), (0, 0), (0, pad))).reshape(Q_LORA, N_HEADS * HEAD_PAD)
    w_qb = jnp.pad(jnp.concatenate([-q_x2, q_x1], axis=2), ((0, 0), (0, 0), (QK_NOPE, pad)))
    w_qb = w_qb.reshape(Q_LORA, N_HEADS * HEAD_PAD)
    return tuple(a.astype(BF16) for a in (w_dkv_p, w_ka, w_kb, w_v, w_qa, w_qb))


def _qkv_kernel(x_ref, gkv_ref, gq_ref, wdkv_ref, glat_ref, wka_ref, wkb_ref, wv_ref, vone_ref, wdq_ref, gql_ref,
                wqa_ref, wqb_ref, cos_ref, sin_ref, q_ref, k_ref, v_ref):
    x = x_ref[...]
    xn = x * lax.rsqrt(jnp.mean(x * x, axis=-1, keepdims=True) + EPS)
    cos = jnp.tile(cos_ref[...].T, (1, N_HEADS))
    sin = jnp.tile(sin_ref[...].T, (1, N_HEADS))
    ckr = _dot((xn * gkv_ref[...]).astype(BF16), wdkv_ref[...])
    latent = _rms(ckr[:, :KV_LORA], glat_ref[...]).astype(BF16)
    rope_raw = ckr[:, KV_LORA:].astype(BF16)
    ka = _dot(jnp.concatenate([latent, rope_raw], axis=1), wka_ref[...])
    kb = _dot(rope_raw, wkb_ref[...])
    k_ref[...] = (ka * cos + kb * sin).astype(BF16)
    v_ref[...] = (_dot(latent, wv_ref[...]) + vone_ref[...]).astype(BF16)
    cq = _rms(_dot((xn * gq_ref[...]).astype(BF16), wdq_ref[...]), gql_ref[...]).astype(BF16)
    qa = _dot(cq, wqa_ref[...])
    qb = _dot(cq, wqb_ref[...])
    q_ref[...] = (qa * cos + qb * sin).astype(BF16)


def qkv_project(x, g_kv, g_q, g_lat, g_qlat, w_dq, mla_w, cos_t, sin_t, rows=512):
    n, d = x.shape
    w_dkv_p, w_ka, w_kb, w_v, w_qa, w_qb = mla_w
    hw = N_HEADS * HEAD_PAD
    lane = jnp.arange(hw, dtype=I32) % (2 * HEAD_PAD)
    v_one = ((lane == ONE_LANE_EVEN) | (lane == HEAD_PAD + ONE_LANE_ODD)).astype(F32).reshape(1, hw)

    def full(a):
        return pl.BlockSpec(a.shape, lambda i: (0,) * a.ndim, pipeline_mode=pl.Buffered(1))

    def row(width):
        return pl.BlockSpec((rows, width), lambda i: (i, 0))

    args = [x, g_kv.reshape(1, d), g_q.reshape(1, d), w_dkv_p, g_lat.reshape(1, KV_LORA), w_ka, w_kb, w_v, v_one,
            w_dq, g_qlat.reshape(1, Q_LORA), w_qa, w_qb, cos_t, sin_t]
    table = pl.BlockSpec((HEAD_PAD, rows), lambda i: (0, i))
    in_specs = [row(d)] + [full(a) for a in args[1:13]] + [table, table]
    return pl.pallas_call(
        _qkv_kernel,
        out_shape=[jax.ShapeDtypeStruct((n, hw), BF16)] * 3,
        grid=(n // rows,),
        in_specs=in_specs,
        out_specs=[row(hw)] * 3,
        compiler_params=_params(1, VMEM_LIMIT),
        name="qkv_project",
    )(*args)


def _attn_kernel(q_ref, k_ref, v_ref, *rest, tq, tk, td, n_cast):
    cast_in, o_ref, cast_out = rest[:n_cast], rest[n_cast], rest[n_cast + 1:2 * n_cast + 1]
    m_ref, acc_ref = rest[2 * n_cast + 1:]
    for src, dst in zip(cast_in, cast_out):
        dst[...] = src[...].astype(dst.dtype)
    qi = pl.program_id(2)
    m_ref[...] = jnp.full_like(m_ref, NEG_BIG)
    acc_ref[...] = jnp.zeros_like(acc_ref)

    def kv_tile(kv0, width, q0, masked):
        rows = pl.ds(pl.multiple_of(kv0, width), width)
        if masked:
            row_chunk = lax.broadcasted_iota(I32, (tq - q0, width), 0) // CHUNK
            col_chunk = lax.broadcasted_iota(I32, (tq - q0, width), 1) // CHUNK
        for h in range(2):
            cols = slice(h * HEAD_PAD, (h + 1) * HEAD_PAD)
            s = lax.dot_general(q_ref[q0:, cols], k_ref[rows, cols], (((1,), (1,)), ((), ())),
                                preferred_element_type=F32)
            if masked:
                s = jnp.where(col_chunk <= row_chunk, s, NEG_BIG)
            m_old = m_ref[h, q0:, :]
            m_new = jnp.maximum(m_old, jnp.max(s, axis=1, keepdims=True))
            p = jnp.exp2(s - jnp.tile(m_new, (1, width // LANES))).astype(BF16)
            acc_ref[h, q0:, :] = jnp.exp2(m_old - m_new) * acc_ref[h, q0:, :] + _dot(p, v_ref[rows, cols])
            m_ref[h, q0:, :] = m_new

    def full_tile(j, carry):
        kv_tile(j * tk, tk, 0, False)
        return carry

    lax.fori_loop(0, qi * (tq // tk), full_tile, 0)
    for d in range(tq // td):
        kv_tile(qi * tq + d * td, td, d * td, True)
    even, odd = acc_ref[0], acc_ref[1]
    even = even / even[:, ONE_LANE_EVEN:ONE_LANE_EVEN + 1]
    odd = odd / odd[:, ONE_LANE_ODD:ONE_LANE_ODD + 1]
    lane = lax.broadcasted_iota(I32, even.shape, 1)
    o_ref[...] = jnp.where(lane < V_HEAD, even, odd).astype(BF16)


def attention(q, k, v, batch, seq, to_bf16=(), tq=1024, tk=1024, td=512):
    tq, tk, td = min(tq, seq), min(tk, seq), min(td, seq)
    n_q = seq // tq
    pair = 2 * HEAD_PAD
    n_pairs = N_HEADS // 2
    n_steps = batch * n_pairs * n_q
    flat = [w.reshape(-1, w.shape[-1]) for w in to_bf16]

    def slice_spec(w):
        return pl.BlockSpec((w.shape[0] // n_steps, w.shape[1]), lambda b, hp, i: ((b * n_pairs + hp) * n_q + i, 0))

    o, *cast = pl.pallas_call(
        functools.partial(_attn_kernel, tq=tq, tk=tk, td=td, n_cast=len(flat)),
        out_shape=[jax.ShapeDtypeStruct((batch * seq, N_HEADS * V_HEAD), BF16)]
        + [jax.ShapeDtypeStruct(w.shape, BF16) for w in flat],
        grid=(batch, n_pairs, n_q),
        in_specs=[
            pl.BlockSpec((tq, pair), lambda b, hp, i: (b * n_q + i, hp)),
            pl.BlockSpec((seq, pair), lambda b, hp, i: (b, hp)),
            pl.BlockSpec((seq, pair), lambda b, hp, i: (b, hp)),
        ] + [slice_spec(w) for w in flat],
        out_specs=[pl.BlockSpec((tq, 2 * V_HEAD), lambda b, hp, i: (b * n_q + i, hp))]
        + [slice_spec(w) for w in flat],
        scratch_shapes=[pltpu.VMEM((2, tq, LANES), F32), pltpu.VMEM((2, tq, HEAD_PAD), F32)],
        compiler_params=_params(3, VMEM_LIMIT),
        name="attention",
    )(q, k, v, *flat)
    return o, [c.reshape(w.shape) for c, w in zip(cast, to_bf16)]


ROUTE_COLS = 8
TOKEN_TILE = SUBLANES


def _to_token_tiles(ref, x):
    rows = x.shape[0]
    for k in range(x.shape[1] // LANES):
        ref[pl.ds(k, rows, stride=TOKEN_TILE), :] = x[:, k * LANES:(k + 1) * LANES]


def _from_token_tiles(ref, rows):
    return jnp.concatenate([ref[pl.ds(k, rows, stride=TOKEN_TILE), :] for k in range(TOKEN_TILE)], axis=1)


def _attn_out_router_kernel(a_ref, x_ref, w_ref, g_ref, rw_ref, tri_ref, x3_ref, h_ref, route_ref, route_t_ref,
                            cnt_ref, carry_ref):
    @pl.when(pl.program_id(0) == 0)
    def _():
        carry_ref[...] = jnp.zeros_like(carry_ref)

    x3 = x_ref[...] + _dot(a_ref[...], w_ref[...])
    x3_ref[...] = x3
    h = _rms(x3, g_ref[...])
    _to_token_tiles(h_ref, h)
    h_hi = h.astype(BF16)
    h_lo = (h - h_hi.astype(F32)).astype(BF16)
    by_hi = _dot(h_hi, rw_ref[...])
    by_lo = _dot(h_lo, rw_ref[...])
    logits = by_hi[:, :N_EXPERTS] + by_hi[:, N_EXPERTS:] + by_lo[:, :N_EXPERTS]
    col = lax.broadcasted_iota(I32, logits.shape, 1)
    m1 = jnp.max(logits, axis=1, keepdims=True)
    i1 = jnp.min(jnp.where(logits == m1, col, N_EXPERTS), axis=1, keepdims=True)
    rest = jnp.where(col == i1, -jnp.inf, logits)
    m2 = jnp.max(rest, axis=1, keepdims=True)
    i2 = jnp.min(jnp.where(rest == m2, col, N_EXPERTS), axis=1, keepdims=True)
    e2 = jnp.exp(m2 - m1)
    w1 = 1.0 / (1.0 + e2)
    w2 = e2 / (1.0 + e2)
    sel = jnp.where((col == i1) | (col == i2), 1.0, 0.0)
    incl = _dot(tri_ref[...], sel.astype(BF16))
    carry = carry_ref[0:1, :]
    excl = incl - sel + carry
    r1 = jnp.sum(jnp.where(col == i1, excl, 0.0), axis=1, keepdims=True)
    r2 = jnp.sum(jnp.where(col == i2, excl, 0.0), axis=1, keepdims=True)
    total = carry + incl[incl.shape[0] - 1:, :]
    carry_ref[...] = jnp.broadcast_to(total, carry_ref.shape)
    cnt_ref[...] = jnp.broadcast_to(total, cnt_ref.shape)
    route = jnp.where(col == 0, i1.astype(F32), 0.0)
    for c, val in ((1, i2.astype(F32)), (2, r1), (3, r2), (4, w1), (5, w2)):
        route = jnp.where(col == c, val, route)
    route_ref[...] = route
    eye = (lax.broadcasted_iota(I32, (ROUTE_COLS, ROUTE_COLS), 0)
           == lax.broadcasted_iota(I32, (ROUTE_COLS, ROUTE_COLS), 1)).astype(F32)
    route_t_ref[...] = lax.dot_general(eye, route, (((1,), (1,)), ((), ())), precision=HIGHEST,
                                       preferred_element_type=F32)


def attn_out_router(a, x, w_o, g, router_w, rows=1024):
    n, d = x.shape
    rows = min(rows, n)
    tri = (jnp.arange(rows)[:, None] >= jnp.arange(rows)[None, :]).astype(BF16)
    rw_hi = router_w.astype(BF16)
    rw_lo = (router_w - rw_hi.astype(F32)).astype(BF16)
    router_w = jnp.concatenate([rw_hi, rw_lo], axis=1)
    return pl.pallas_call(
        _attn_out_router_kernel,
        out_shape=[jax.ShapeDtypeStruct((n, d), F32), jax.ShapeDtypeStruct((n * TOKEN_TILE, LANES), F32),
                   jax.ShapeDtypeStruct((n, ROUTE_COLS), F32), jax.ShapeDtypeStruct((ROUTE_COLS, n), F32),
                   jax.ShapeDtypeStruct((8, N_EXPERTS), F32)],
        grid=(n // rows,),
        in_specs=[
            pl.BlockSpec((rows, a.shape[1]), lambda i: (i, 0)),
            pl.BlockSpec((rows, d), lambda i: (i, 0)),
            pl.BlockSpec(w_o.shape, lambda i: (0, 0)),
            pl.BlockSpec((1, d), lambda i: (0, 0)),
            pl.BlockSpec((d, 2 * N_EXPERTS), lambda i: (0, 0)),
            pl.BlockSpec((rows, rows), lambda i: (0, 0)),
        ],
        out_specs=[
            pl.BlockSpec((rows, d), lambda i: (i, 0)),
            pl.BlockSpec((rows * TOKEN_TILE, LANES), lambda i: (i, 0)),
            pl.BlockSpec((rows, ROUTE_COLS), lambda i: (i, 0)),
            pl.BlockSpec((ROUTE_COLS, rows), lambda i: (0, i)),
            pl.BlockSpec((8, N_EXPERTS), lambda i: (0, 0)),
        ],
        scratch_shapes=[pltpu.VMEM((8, N_EXPERTS), F32)],
        compiler_params=_params(1, VMEM_LIMIT),
        name="attn_out_router",
    )(a, x, w_o, g.reshape(1, d), router_w, tri)


def _tile_copy(src_ref, src_token, dst_ref, dst_token, sem):
    src = src_ref.at[pl.ds(pl.multiple_of(src_token * TOKEN_TILE, TOKEN_TILE), TOKEN_TILE)]
    dst = dst_ref.at[pl.ds(pl.multiple_of(dst_token * TOKEN_TILE, TOKEN_TILE), TOKEN_TILE)]
    return pltpu.make_async_copy(src, dst, sem)


ISSUE_UNROLL = 8


def _dispatch_kernel(zoff_ref, nt_ref, p0_ref, p1_ref, h_ref, xs_ref, zero_ref, sem, zsem, *,
                     rows, tile_rows, n_buf_tiles):
    tile_words = tile_rows * TOKEN_TILE

    @pl.when(pl.program_id(0) == 0)
    def _():
        zero_ref[...] = jnp.zeros_like(zero_ref)
        copies = [pltpu.make_async_copy(
            zero_ref, xs_ref.at[pl.ds(pl.multiple_of(zoff_ref[e] * TOKEN_TILE, TOKEN_TILE), tile_words)], zsem)
            for e in range(N_EXPERTS)]
        for c in copies:
            c.start()
        for c in copies:
            c.wait()
        def fill_idle(t, carry):
            c = pltpu.make_async_copy(
                zero_ref, xs_ref.at[pl.ds(pl.multiple_of(t * tile_words, tile_words), tile_words)], zsem)
            c.start()
            c.wait()
            return carry

        lax.fori_loop(nt_ref[0], n_buf_tiles, fill_idle, 0)

    def issue(r, carry):
        _tile_copy(h_ref, r, xs_ref, p0_ref[r], sem).start(priority=0)
        _tile_copy(h_ref, r, xs_ref, p1_ref[r], sem).start(priority=1)
        return carry

    lax.fori_loop(0, rows, issue, 0, unroll=ISSUE_UNROLL)
    for _ in range(2):
        pltpu.make_async_copy(h_ref, xs_ref.at[pl.ds(0, rows * TOKEN_TILE)], sem).wait()


def dispatch(h, pos0, pos1, zero_off, n_tiles, n_buf_tiles, tile_rows, rows=1024):
    n = h.shape[0] // TOKEN_TILE
    rows = min(rows, n)
    n_steps = n // rows
    grid_spec = pltpu.PrefetchScalarGridSpec(
        num_scalar_prefetch=2,
        grid=(n_steps,),
        in_specs=[
            pl.BlockSpec((rows,), lambda i, z, t: (i,), memory_space=pltpu.SMEM),
            pl.BlockSpec((rows,), lambda i, z, t: (i,), memory_space=pltpu.SMEM),
            pl.BlockSpec((rows * TOKEN_TILE, LANES), lambda i, z, t: (i, 0)),
        ],
        out_specs=pl.BlockSpec(memory_space=pl.ANY),
        scratch_shapes=[pltpu.VMEM((tile_rows * TOKEN_TILE, LANES), F32), pltpu.SemaphoreType.DMA,
                        pltpu.SemaphoreType.DMA],
    )
    return pl.pallas_call(
        functools.partial(_dispatch_kernel, rows=rows, tile_rows=tile_rows, n_buf_tiles=n_buf_tiles),
        out_shape=jax.ShapeDtypeStruct((n_buf_tiles * tile_rows * TOKEN_TILE, LANES), F32),
        grid_spec=grid_spec,
        compiler_params=_params(1, VMEM_LIMIT),
        name="dispatch",
    )(zero_off, n_tiles, pos0, pos1, h)


def _combine_kernel(p0_ref, p1_ref, q0_ref, q1_ref, ys_ref, x_ref, w_ref, g_ref, o_ref, buf, sem, *, rows):
    i = pl.program_id(0)
    slot = i % 2

    def issue(a_ref, b_ref, to_slot):
        def body(r, carry):
            _tile_copy(ys_ref, a_ref[r], buf.at[to_slot, 0], r, sem.at[to_slot]).start(priority=0)
            _tile_copy(ys_ref, b_ref[r], buf.at[to_slot, 1], r, sem.at[to_slot]).start(priority=1)
            return carry

        lax.fori_loop(0, rows, body, 0, unroll=ISSUE_UNROLL)

    @pl.when(i == 0)
    def _():
        issue(p0_ref, p1_ref, 0)

    @pl.when(i + 1 < pl.num_programs(0))
    def _():
        issue(q0_ref, q1_ref, 1 - slot)

    for k in range(2):
        pltpu.make_async_copy(ys_ref.at[pl.ds(0, rows * TOKEN_TILE)], buf.at[slot, k], sem.at[slot]).wait()
    w = w_ref[...]
    y = (x_ref[...] + w[:, 4:5] * _from_token_tiles(buf.at[slot, 0], rows)
         + w[:, 5:6] * _from_token_tiles(buf.at[slot, 1], rows))
    o_ref[...] = _rms(y, g_ref[...])


def combine(ys, pos0, pos1, x, route, g, rows=512):
    n, d = x.shape
    rows = min(rows, n)
    n_blocks = n // rows

    def cur(i):
        return (i,)

    def nxt(i):
        return (jnp.minimum(i + 1, n_blocks - 1),)

    return pl.pallas_call(
        functools.partial(_combine_kernel, rows=rows),
        out_shape=jax.ShapeDtypeStruct((n, d), F32),
        grid=(n_blocks,),
        in_specs=[
            pl.BlockSpec((rows,), cur, memory_space=pltpu.SMEM),
            pl.BlockSpec((rows,), cur, memory_space=pltpu.SMEM),
            pl.BlockSpec((rows,), nxt, memory_space=pltpu.SMEM),
            pl.BlockSpec((rows,), nxt, memory_space=pltpu.SMEM),
            pl.BlockSpec(memory_space=pl.ANY),
            pl.BlockSpec((rows, d), lambda i: (i, 0)),
            pl.BlockSpec((rows, ROUTE_COLS), lambda i: (i, 0)),
            pl.BlockSpec((1, d), lambda i: (0, 0)),
        ],
        out_specs=pl.BlockSpec((rows, d), lambda i: (i, 0)),
        scratch_shapes=[pltpu.VMEM((2, 2, rows * TOKEN_TILE, LANES), F32), pltpu.SemaphoreType.DMA((2,))],
        compiler_params=_params(1, VMEM_LIMIT),
        name="combine",
    )(pos0, pos1, pos0, pos1, ys, x, route, g.reshape(1, d))


def moe_layer(attn, x2, w_o, g_ffn, g_final, router_w, wg, wu, wd, tile_rows=512):
    n, d = x2.shape
    tile_rows = min(tile_rows, n)
    x3, h, route, route_t, counts = attn_out_router(attn, x2, w_o, g_ffn, router_w)
    counts = counts[0].astype(I32)
    tiles = (counts + tile_rows - 1) // tile_rows
    tile_start = jnp.cumsum(tiles) - tiles
    row_start = tile_start * tile_rows
    n_tiles = jnp.sum(tiles).reshape(1).astype(I32)
    max_tiles = (2 * n) // tile_rows + N_EXPERTS
    n_buf_tiles = max_tiles + 1
    e0, e1, rank0, rank1 = (route_t[c].astype(I32) for c in range(4))
    experts = jnp.arange(N_EXPERTS, dtype=I32)[:, None]

    def start_of(e):
        return jnp.sum(jnp.where(e[None, :] == experts, row_start[:, None], 0), axis=0)

    pos0 = start_of(e0) + rank0
    pos1 = start_of(e1) + rank1
    t_idx = jnp.arange(max_tiles, dtype=I32)
    tile_expert = jnp.sum(t_idx[:, None] >= (tile_start + tiles)[None, :], axis=1).astype(I32)
    tile_expert = jnp.where(t_idx < n_tiles[0], jnp.minimum(tile_expert, N_EXPERTS - 1),
                            tile_expert[n_tiles[0] - 1])
    zero_off = (row_start + counts).astype(I32)
    xs = dispatch(h, pos0, pos1, zero_off, n_tiles, n_buf_tiles, tile_rows)
    ys = moe_experts(xs, tile_expert, n_tiles, wg, wu, wd, max_tiles, tile_rows)
    return combine(ys, pos0, pos1, x3, route, g_final)


def kernel(x, positions, norm_mix, norm_ffn, final_norm, s5_w_in, s5_lambda_re, s5_lambda_im, s5_log_dt, s5_b_re, s5_b_im, s5_c_re, s5_c_im, s5_d, s5_w_glu, s5_w_out, kv_norm, w_dkv, kv_latent_norm, w_ukv, w_dq, q_latent_norm, w_uq, w_o, ffn_w_gate, ffn_w_up, ffn_w_down, router_w, moe_w_gate, moe_w_up, moe_w_down):
    batch, seq, d = x.shape
    n = batch * seq
    x0 = x.reshape(n, d)
    u = norm_matmul_chunked(x0, norm_mix[0], s5_w_in[0].astype(BF16), BF16, S5_T)
    s5_w = _s5_weights(s5_lambda_re[0], s5_lambda_im[0], s5_log_dt[0], s5_b_re[0], s5_b_im[0],
                       s5_c_re[0], s5_c_im[0])
    y = s5_scan(u, *s5_w, s5_d[0], batch, seq).reshape(n, d)
    x2 = s5_out_ffn(y, x0, s5_w_glu[0].astype(BF16), s5_w_out[0].astype(BF16), norm_ffn[0],
                    ffn_w_gate[0].astype(BF16), ffn_w_up[0].astype(BF16), ffn_w_down[0].astype(BF16))
    cos_tab, sin_tab = rope_tables(positions.reshape(n))
    mla_w = _mla_weights(w_dkv, w_ukv, w_uq[0])
    q, k, v = qkv_project(x2, kv_norm, norm_mix[1], kv_latent_norm, q_latent_norm[0], w_dq[0].astype(BF16),
                          mla_w, cos_tab, sin_tab)
    o, moe_w = attention(q, k, v, batch, seq, to_bf16=(moe_w_gate[0], moe_w_up[0], moe_w_down[0]))
    out = moe_layer(o, x2, w_o[0].astype(BF16), norm_ffn[1], final_norm, router_w[0], *moe_w)
    return out.reshape(batch, seq, d)
```

```python
import functools

import jax
import jax.numpy as jnp
from jax import lax
from jax.experimental import pallas as pl
from jax.experimental.pallas import tpu as pltpu

F32 = jnp.float32
BF16 = jnp.bfloat16
I32 = jnp.int32
HIGHEST = lax.Precision.HIGHEST

D_MODEL = 1024
CHUNK = 64
SSM_GROUP = 16
SSM_GROUPS = D_MODEL // SSM_GROUP
SSM_STATE = 64
N_HEADS = 16
QK_NOPE = 64
QK_ROPE = 32
V_HEAD = 64
Q_LORA = 512
KV_LORA = 256
ROPE_BASE = 10000.0
D_FF = 2688
N_EXPERTS = 8
MOE_FF = 3584
EPS = 1e-6

LANES = 128
SUBLANES = 8
HEAD_PAD = 128
S5_T = 8
S5_LANE_GROUPS = LANES // SSM_GROUP
S5_BLOCKS = D_MODEL // LANES
S5_STATE_W = S5_LANE_GROUPS * SSM_STATE
NEG_BIG = -1e30
LOG2_E = 1.4426950408889634
ONE_LANE_EVEN = V_HEAD
ONE_LANE_ODD = 0
VMEM_LIMIT = 56 * 1024 * 1024
MOE_VMEM_LIMIT = 62 * 1024 * 1024


def _params(n_axes, vmem=None):
    return pltpu.CompilerParams(dimension_semantics=("arbitrary",) * n_axes, vmem_limit_bytes=vmem)


def _rms(x, g):
    return x * lax.rsqrt(jnp.mean(x * x, axis=-1, keepdims=True) + EPS) * g


def _dot(a, b):
    return jnp.dot(a, b, preferred_element_type=F32)


def _norm_matmul_chunked_kernel(x_ref, g_ref, w_ref, o_ref, slab_ref, *, t_steps):
    h = _rms(x_ref[...], g_ref[...]).astype(BF16)
    res = _dot(h, w_ref[...])
    rows, dout = res.shape
    for k in range(dout // LANES):
        slab_ref[k] = res[:, k * LANES:(k + 1) * LANES]
    for s in range(t_steps):
        for k in range(dout // LANES):
            lanes = slice(s * dout + k * LANES, s * dout + (k + 1) * LANES)
            o_ref[:, lanes] = slab_ref[k, pl.ds(s, rows // t_steps, stride=t_steps), :].astype(o_ref.dtype)


def norm_matmul_chunked(x, g, w, out_dtype, t_steps, rows=512):
    n, din = x.shape
    dout = w.shape[1]
    return pl.pallas_call(
        functools.partial(_norm_matmul_chunked_kernel, t_steps=t_steps),
        out_shape=jax.ShapeDtypeStruct((n // t_steps, t_steps * dout), out_dtype),
        grid=(n // rows,),
        in_specs=[
            pl.BlockSpec((rows, din), lambda i: (i, 0)),
            pl.BlockSpec((1, din), lambda i: (0, 0)),
            pl.BlockSpec((din, dout), lambda i: (0, 0)),
        ],
        out_specs=pl.BlockSpec((rows // t_steps, t_steps * dout), lambda i: (i, 0)),
        scratch_shapes=[pltpu.VMEM((dout // LANES, rows, LANES), F32)],
        compiler_params=_params(1),
        name="norm_matmul_chunked",
    )(x, g.reshape(1, din), w)


def _s5_weights(lam_re, lam_im, log_dt, b_re, b_im, c_re, c_im):
    t_steps = S5_T
    lr, li = lam_re.astype(F32), lam_im.astype(F32)
    dt = jnp.exp(log_dt.astype(F32))[:, None]
    mag = jnp.exp(lr * dt)
    ab_re, ab_im = mag * jnp.cos(li * dt), mag * jnp.sin(li * dt)
    den = lr * lr + li * li
    nr, ni = ab_re - 1.0, ab_im
    coef_re = (nr * lr + ni * li) / den
    coef_im = (ni * lr - nr * li) / den
    br, bi = b_re.astype(F32), b_im.astype(F32)
    bb_re = coef_re[..., None] * br - coef_im[..., None] * bi
    bb_im = coef_re[..., None] * bi + coef_im[..., None] * br
    pr, pi = [jnp.ones_like(ab_re)], [jnp.zeros_like(ab_im)]
    for _ in range(t_steps):
        pr.append(pr[-1] * ab_re - pi[-1] * ab_im)
        pi.append(pr[-2] * ab_im + pi[-1] * ab_re)
    pw_re, pw_im = jnp.stack(pr), jnp.stack(pi)
    cr, ci = c_re.astype(F32), c_im.astype(F32)
    nb, lg = S5_BLOCKS, S5_LANE_GROUPS

    def blocks(a):
        return a.reshape(a.shape[0], nb, lg, *a.shape[2:])

    def widen(small, row_group, col_of, group_of_col):
        n_small, n_wide = small.shape[-1], col_of.shape[0]
        place = (jnp.arange(n_small, dtype=I32)[:, None] == col_of[None, :]).astype(BF16)
        wide = jnp.dot(small.reshape(-1, n_small).astype(BF16), place, preferred_element_type=F32)
        keep = row_group[:, None] == group_of_col[None, :]
        return jnp.where(keep[None], wide.reshape(nb, -1, n_wide), 0.0).astype(BF16)

    dec_re = jnp.stack([pr[t_steps - 1 - s] for s in range(t_steps)])
    dec_im = jnp.stack([pi[t_steps - 1 - s] for s in range(t_steps)])
    si_re = dec_re[..., None] * bb_re[None] - dec_im[..., None] * bb_im[None]
    si_im = dec_re[..., None] * bb_im[None] + dec_im[..., None] * bb_re[None]
    small = jnp.stack([jnp.transpose(blocks(a), (1, 0, 2, 4, 3)) for a in (si_re, si_im)], axis=4)
    small = small.reshape(nb, t_steps * LANES, 2 * SSM_STATE)
    wide_col = jnp.arange(2 * S5_STATE_W, dtype=I32)
    w_in = widen(small, (jnp.arange(t_steps * LANES, dtype=I32) // SSM_GROUP) % lg,
                 (wide_col // S5_STATE_W) * SSM_STATE + wide_col % SSM_STATE, (wide_col // SSM_STATE) % lg)
    up_re, up_im = pw_re[1:t_steps + 1], pw_im[1:t_steps + 1]
    so_re = cr[None] * up_re[:, :, None, :] - ci[None] * up_im[:, :, None, :]
    so_im = cr[None] * up_im[:, :, None, :] + ci[None] * up_re[:, :, None, :]
    small = jnp.stack([jnp.transpose(blocks(a), (1, 2, 4, 0, 3)) for a in (so_re, -so_im)], axis=1)
    small = small.reshape(nb, 2 * S5_STATE_W, t_steps * SSM_GROUP)
    wide_col = jnp.arange(t_steps * LANES, dtype=I32)
    w_out = widen(small, (jnp.arange(2 * S5_STATE_W, dtype=I32) // SSM_STATE) % lg,
                  (wide_col // LANES) * SSM_GROUP + wide_col % SSM_GROUP, (wide_col // SSM_GROUP) % lg)
    lg_re = cr[None] * pw_re[:t_steps, :, None, :] - ci[None] * pw_im[:t_steps, :, None, :]
    lg_im = cr[None] * pw_im[:t_steps, :, None, :] + ci[None] * pw_re[:t_steps, :, None, :]
    bt_re, bt_im = jnp.swapaxes(bb_re, 1, 2), jnp.swapaxes(bb_im, 1, 2)
    k_lag = jnp.sum(lg_re[:, :, None, :, :] * bt_re[None, :, :, None, :]
                    - lg_im[:, :, None, :, :] * bt_im[None, :, :, None, :], axis=-1)
    small = jnp.transpose(blocks(k_lag), (1, 0, 2, 3, 4)).reshape(nb, t_steps * LANES, SSM_GROUP)
    wide_col = jnp.arange(LANES, dtype=I32)
    w_lag = widen(small, (jnp.arange(t_steps * LANES, dtype=I32) // SSM_GROUP) % lg,
                  wide_col % SSM_GROUP, wide_col // SSM_GROUP).reshape(nb, t_steps, LANES, LANES)
    zero = jnp.zeros((nb, LANES, LANES), BF16)
    pairs = []
    for d in range(t_steps // 2):
        rows = []
        for s_local in range(2):
            lag = [2 * d + t_local - s_local for t_local in range(2)]
            rows.append(jnp.concatenate([w_lag[:, j] if j >= 0 else zero for j in lag], axis=2))
        pairs.append(jnp.concatenate(rows, axis=1))
    w_pairs = jnp.stack(pairs, axis=1)
    lam_t = jnp.concatenate([blocks(pw_re[t_steps][None])[0].reshape(nb, 1, S5_STATE_W),
                             blocks(pw_im[t_steps][None])[0].reshape(nb, 1, S5_STATE_W)], axis=2)
    return w_in, w_out, w_pairs, lam_t


def _s5_kernel(*refs, batch, rows_per_batch):
    t_steps = S5_T
    u_refs = refs[:t_steps]
    w_in_ref, w_out_ref, w_pair_ref, lam_ref, d_ref, y_ref, state_ref, carry_ref = refs[t_steps:]
    n_rows = batch * rows_per_batch
    half = S5_STATE_W

    @pl.when(pl.program_id(1) == 0)
    def _():
        carry_ref[...] = jnp.zeros_like(carry_ref)

    us = [r[...].reshape(n_rows, LANES) for r in u_refs]
    u_all = jnp.concatenate(us, axis=1)
    n_slabs = 2 * half // LANES
    contrib = _dot(u_all, w_in_ref[0])
    pitch = rows_per_batch + SUBLANES
    for k in range(n_slabs):
        for b in range(batch):
            state_ref[k, b * pitch:b * pitch + rows_per_batch, :] = (
                contrib[b * rows_per_batch:(b + 1) * rows_per_batch, k * LANES:(k + 1) * LANES])

    lam = lam_ref[0]
    a_re = jnp.broadcast_to(lam[:, :half], (batch, half))
    a_im = jnp.broadcast_to(lam[:, half:], (batch, half))

    def step(c, h):
        rows = pl.ds(c, batch, stride=pitch)
        x = jnp.concatenate([state_ref[k, rows, :] for k in range(n_slabs)], axis=1)
        for k in range(n_slabs):
            state_ref[k, rows, :] = h[:, k * LANES:(k + 1) * LANES]
        h_re, h_im = h[:, :half], h[:, half:]
        n_re = a_re * h_re - a_im * h_im + x[:, :half]
        n_im = a_re * h_im + a_im * h_re + x[:, half:]
        return jnp.concatenate([n_re, n_im], axis=1)

    carry_ref[...] = lax.fori_loop(0, rows_per_batch, step, carry_ref[...])

    entering = jnp.concatenate(
        [jnp.concatenate([state_ref[k, b * pitch:b * pitch + rows_per_batch, :] for b in range(batch)], axis=0)
         for k in range(n_slabs)], axis=1).astype(BF16)
    y_state = _dot(entering, w_out_ref[0])
    d_skip = d_ref[...]
    for a in range(t_steps // 2):
        acc = y_state[:, 2 * a * LANES:(2 * a + 2) * LANES]
        for b in range(a + 1):
            acc = acc + _dot(u_all[:, 2 * b * LANES:(2 * b + 2) * LANES], w_pair_ref[0, a - b])
        for t_local in range(2):
            t = 2 * a + t_local
            y = acc[:, t_local * LANES:(t_local + 1) * LANES] + d_skip * us[t].astype(F32)
            y_ref[:, pl.ds(t, rows_per_batch, stride=t_steps), :] = (
                jax.nn.gelu(y).reshape(batch, rows_per_batch, LANES))


def s5_scan(u, w_in, w_out, w_pairs, lam_t, d_skip, batch, seq):
    t_steps = S5_T
    n_chunk_rows = seq // t_steps
    rows_per_batch = min(128, n_chunk_rows)
    u3 = u.reshape(batch, n_chunk_rows, t_steps * D_MODEL)
    blk = (batch, rows_per_batch, LANES)

    def u_spec(s):
        return pl.BlockSpec(blk, lambda j, c, s=s: (0, c, s * S5_BLOCKS + j))

    return pl.pallas_call(
        functools.partial(_s5_kernel, batch=batch, rows_per_batch=rows_per_batch),
        out_shape=jax.ShapeDtypeStruct((batch, seq, D_MODEL), F32),
        grid=(S5_BLOCKS, n_chunk_rows // rows_per_batch),
        in_specs=[u_spec(s) for s in range(t_steps)] + [
            pl.BlockSpec((1, t_steps * LANES, 2 * S5_STATE_W), lambda j, c: (j, 0, 0)),
            pl.BlockSpec((1, 2 * S5_STATE_W, t_steps * LANES), lambda j, c: (j, 0, 0)),
            pl.BlockSpec((1, t_steps // 2, 2 * LANES, 2 * LANES), lambda j, c: (j, 0, 0, 0)),
            pl.BlockSpec((1, 1, 2 * S5_STATE_W), lambda j, c: (j, 0, 0)),
            pl.BlockSpec((1, LANES), lambda j, c: (0, j)),
        ],
        out_specs=pl.BlockSpec((batch, rows_per_batch * t_steps, LANES), lambda j, c: (0, c, j)),
        scratch_shapes=[
            pltpu.VMEM((2 * S5_STATE_W // LANES, batch * (rows_per_batch + SUBLANES), LANES), F32),
            pltpu.VMEM((batch, 2 * S5_STATE_W), F32),
        ],
        compiler_params=_params(2, VMEM_LIMIT),
        name="s5_scan",
    )(*([u3] * t_steps), w_in, w_out, w_pairs, lam_t, d_skip.reshape(1, D_MODEL))


def _s5_out_ffn_kernel(y_ref, x_ref, w_glu_ref, w_out_ref, g_ref, wg_ref, wu_ref, wd_ref, o_ref):
    y = y_ref[...]
    gate = jax.nn.sigmoid(_dot(y.astype(BF16), w_glu_ref[...]))
    x1 = x_ref[...] + _dot((y * gate).astype(BF16), w_out_ref[...])
    xn = _rms(x1, g_ref[...]).astype(BF16)
    mid = (jax.nn.silu(_dot(xn, wg_ref[...])) * _dot(xn, wu_ref[...])).astype(BF16)
    o_ref[...] = x1 + _dot(mid, wd_ref[...])


def s5_out_ffn(y, x, w_glu, w_out, g, wg, wu, wd, rows=512):
    n, d = x.shape

    def resident(a):
        return pl.BlockSpec(a.shape, lambda i: (0, 0), pipeline_mode=pl.Buffered(1))

    def row():
        return pl.BlockSpec((rows, d), lambda i: (i, 0))

    g = g.reshape(1, d)
    return pl.pallas_call(
        _s5_out_ffn_kernel,
        out_shape=jax.ShapeDtypeStruct((n, d), F32),
        grid=(n // rows,),
        in_specs=[row(), row()] + [resident(a) for a in (w_glu, w_out, g, wg, wu, wd)],
        out_specs=row(),
        compiler_params=_params(1, VMEM_LIMIT),
        name="s5_out_ffn",
    )(y, x, w_glu, w_out, g, wg, wu, wd)


def _moe_kernel(te_ref, nt_ref, xs_ref, wg_ref, wu_ref, wd_ref, o_ref, *, rows):
    t = pl.program_id(0)

    @pl.when(t < nt_ref[0])
    def _():
        xn = _from_token_tiles(xs_ref, rows).astype(BF16)
        mid = (jax.nn.silu(_dot(xn, wg_ref[0])) * _dot(xn, wu_ref[0])).astype(BF16)
        _to_token_tiles(o_ref, _dot(mid, wd_ref[0]))

    @pl.when(t >= nt_ref[0])
    def _():
        o_ref[...] = jnp.zeros_like(o_ref)


def moe_experts(xs, tile_expert, n_tiles, wg, wu, wd, n_grid_tiles, rows):
    def resident(w):
        return pl.BlockSpec((1,) + w.shape[1:], lambda t, te, nt: (te[t], 0, 0), pipeline_mode=pl.Buffered(2))

    tokens = pl.BlockSpec((rows * TOKEN_TILE, LANES), lambda t, te, nt: (t, 0))

    grid_spec = pltpu.PrefetchScalarGridSpec(
        num_scalar_prefetch=2,
        grid=(n_grid_tiles,),
        in_specs=[tokens, resident(wg), resident(wu), resident(wd)],
        out_specs=tokens,
    )
    return pl.pallas_call(
        functools.partial(_moe_kernel, rows=rows),
        out_shape=jax.ShapeDtypeStruct((n_grid_tiles * rows * TOKEN_TILE, LANES), F32),
        grid_spec=grid_spec,
        compiler_params=_params(1, MOE_VMEM_LIMIT),
        name="moe_experts",
    )(tile_expert, n_tiles, xs, wg, wu, wd)


def _rope_kernel(pos_ref, freq_ref, cos_ref, sin_ref):
    ang = freq_ref[...] * pos_ref[...]
    cos, sin = jnp.cos(ang), jnp.sin(ang)
    n = ang.shape[1]
    pad = HEAD_PAD - QK_NOPE - QK_ROPE
    cos_ref[...] = jnp.concatenate([jnp.ones((QK_NOPE, n), F32), cos, cos, jnp.zeros((pad, n), F32)], axis=0)
    sin_ref[...] = jnp.concatenate([jnp.zeros((QK_NOPE, n), F32), sin, sin, jnp.zeros((pad, n), F32)], axis=0)


def rope_tables(positions):
    n = positions.shape[0]
    half = QK_ROPE // 2
    inv_freq = ROPE_BASE ** (-jnp.arange(half, dtype=F32) * (2.0 / QK_ROPE))
    tile = min(n, 4096)
    return pl.pallas_call(
        _rope_kernel,
        out_shape=[jax.ShapeDtypeStruct((HEAD_PAD, n), F32)] * 2,
        grid=(n // tile,),
        in_specs=[pl.BlockSpec((1, tile), lambda i: (0, i)), pl.BlockSpec((half, 1), lambda i: (0, 0))],
        out_specs=[pl.BlockSpec((HEAD_PAD, tile), lambda i: (0, i))] * 2,
        compiler_params=_params(1),
        name="rope_tables",
    )(positions.astype(F32).reshape(1, n), inv_freq.reshape(half, 1))


def _mla_weights(w_dkv, w_ukv, w_uq):
    half = QK_ROPE // 2
    pad = HEAD_PAD - QK_NOPE - QK_ROPE
    scale = (QK_NOPE + QK_ROPE) ** -0.5
    w_dkv_p = jnp.pad(w_dkv, ((0, 0), (0, LANES - QK_ROPE)))
    ukv = w_ukv.reshape(KV_LORA, N_HEADS, QK_NOPE + V_HEAD)
    k_nope = jnp.pad(ukv[:, :, :QK_NOPE], ((0, 0), (0, 0), (0, HEAD_PAD - QK_NOPE)))
    eye = jnp.eye(half, dtype=F32)
    zer = jnp.zeros((half, half), F32)

    def place(x1_to, x2_to):
        blk = jnp.concatenate([jnp.concatenate([x1_to[0], x1_to[1]], axis=1),
                               jnp.concatenate([x2_to[0], x2_to[1]], axis=1)], axis=0)
        return jnp.pad(blk, ((0, 0), (QK_NOPE, pad)))

    rope_a = place((eye, zer), (zer, eye))
    rope_b = place((zer, eye), (-eye, zer))
    rope_a = jnp.tile(rope_a[:, None, :], (1, N_HEADS, 1))
    rope_b = jnp.tile(rope_b[:, None, :], (1, N_HEADS, 1))
    zrows = jnp.zeros((LANES - QK_ROPE, N_HEADS, HEAD_PAD), F32)
    w_ka = jnp.concatenate([k_nope, rope_a, zrows], axis=0).reshape(KV_LORA + LANES, N_HEADS * HEAD_PAD)
    w_kb = jnp.concatenate([rope_b, zrows], axis=0).reshape(LANES, N_HEADS * HEAD_PAD)
    vh = ukv[:, :, QK_NOPE:].reshape(KV_LORA, N_HEADS // 2, 2, V_HEAD)
    w_v = jnp.concatenate([jnp.pad(vh[:, :, 0], ((0, 0), (0, 0), (0, HEAD_PAD - V_HEAD))),
                           jnp.pad(vh[:, :, 1], ((0, 0), (0, 0), (HEAD_PAD - V_HEAD, 0)))], axis=2)
    w_v = w_v.reshape(KV_LORA, N_HEADS * HEAD_PAD)
    uq = w_uq.reshape(Q_LORA, N_HEADS, QK_NOPE + QK_ROPE) * (scale * LOG2_E)
    q_x1, q_x2 = uq[:, :, QK_NOPE:QK_NOPE + half], uq[:, :, QK_NOPE + half:]
    w_qa = jnp.pad(uq, ((0, 0), (0, 0), (0, pad))).reshape(Q_LORA, N_HEADS * HEAD_PAD)
    w_qb = jnp.pad(jnp.concatenate([-q_x2, q_x1], axis=2), ((0, 0), (0, 0), (QK_NOPE, pad)))
    w_qb = w_qb.reshape(Q_LORA, N_HEADS * HEAD_PAD)
    return tuple(a.astype(BF16) for a in (w_dkv_p, w_ka, w_kb, w_v, w_qa, w_qb))


def _qkv_kernel(x_ref, gkv_ref, gq_ref, wdkv_ref, glat_ref, wka_ref, wkb_ref, wv_ref, vone_ref, wdq_ref, gql_ref,
                wqa_ref, wqb_ref, cos_ref, sin_ref, q_ref, k_ref, v_ref):
    x = x_ref[...]
    xn = x * lax.rsqrt(jnp.mean(x * x, axis=-1, keepdims=True) + EPS)
    cos = jnp.tile(cos_ref[...].T, (1, N_HEADS))
    sin = jnp.tile(sin_ref[...].T, (1, N_HEADS))
    ckr = _dot((xn * gkv_ref[...]).astype(BF16), wdkv_ref[...])
    latent = _rms(ckr[:, :KV_LORA], glat_ref[...]).astype(BF16)
    rope_raw = ckr[:, KV_LORA:].astype(BF16)
    ka = _dot(jnp.concatenate([latent, rope_raw], axis=1), wka_ref[...])
    kb = _dot(rope_raw, wkb_ref[...])
    k_ref[...] = (ka * cos + kb * sin).astype(BF16)
    v_ref[...] = (_dot(latent, wv_ref[...]) + vone_ref[...]).astype(BF16)
    cq = _rms(_dot((xn * gq_ref[...]).astype(BF16), wdq_ref[...]), gql_ref[...]).astype(BF16)
    qa = _dot(cq, wqa_ref[...])
    qb = _dot(cq, wqb_ref[...])
    q_ref[...] = (qa * cos + qb * sin).astype(BF16)


def qkv_project(x, g_kv, g_q, g_lat, g_qlat, w_dq, mla_w, cos_t, sin_t, rows=512):
    n, d = x.shape
    w_dkv_p, w_ka, w_kb, w_v, w_qa, w_qb = mla_w
    hw = N_HEADS * HEAD_PAD
    lane = jnp.arange(hw, dtype=I32) % (2 * HEAD_PAD)
    v_one = ((lane == ONE_LANE_EVEN) | (lane == HEAD_PAD + ONE_LANE_ODD)).astype(F32).reshape(1, hw)

    def full(a):
        return pl.BlockSpec(a.shape, lambda i: (0,) * a.ndim, pipeline_mode=pl.Buffered(1))

    def row(width):
        return pl.BlockSpec((rows, width), lambda i: (i, 0))

    args = [x, g_kv.reshape(1, d), g_q.reshape(1, d), w_dkv_p, g_lat.reshape(1, KV_LORA), w_ka, w_kb, w_v, v_one,
            w_dq, g_qlat.reshape(1, Q_LORA), w_qa, w_qb, cos_t, sin_t]
    table = pl.BlockSpec((HEAD_PAD, rows), lambda i: (0, i))
    in_specs = [row(d)] + [full(a) for a in args[1:13]] + [table, table]
    return pl.pallas_call(
        _qkv_kernel,
        out_shape=[jax.ShapeDtypeStruct((n, hw), BF16)] * 3,
        grid=(n // rows,),
        in_specs=in_specs,
        out_specs=[row(hw)] * 3,
        compiler_params=_params(1, VMEM_LIMIT),
        name="qkv_project",
    )(*args)


def _attn_kernel(q_ref, k_ref, v_ref, *rest, tq, tk, td, n_cast):
    cast_in, o_ref, cast_out = rest[:n_cast], rest[n_cast], rest[n_cast + 1:2 * n_cast + 1]
    m_ref, acc_ref = rest[2 * n_cast + 1:]
    for src, dst in zip(cast_in, cast_out):
        dst[...] = src[...].astype(dst.dtype)
    qi = pl.program_id(2)
    m_ref[...] = jnp.full_like(m_ref, NEG_BIG)
    acc_ref[...] = jnp.zeros_like(acc_ref)

    def kv_tile(kv0, width, q0, masked):
        rows = pl.ds(pl.multiple_of(kv0, width), width)
        if masked:
            row_chunk = lax.broadcasted_iota(I32, (tq - q0, width), 0) // CHUNK
            col_chunk = lax.broadcasted_iota(I32, (tq - q0, width), 1) // CHUNK
        for h in range(2):
            cols = slice(h * HEAD_PAD, (h + 1) * HEAD_PAD)
            s = lax.dot_general(q_ref[q0:, cols], k_ref[rows, cols], (((1,), (1,)), ((), ())),
                                preferred_element_type=F32)
            if masked:
                s = jnp.where(col_chunk <= row_chunk, s, NEG_BIG)
            m_old = m_ref[h, q0:, :]
            m_new = jnp.maximum(m_old, jnp.max(s, axis=1, keepdims=True))
            p = jnp.exp2(s - jnp.tile(m_new, (1, width // LANES))).astype(BF16)
            acc_ref[h, q0:, :] = jnp.exp2(m_old - m_new) * acc_ref[h, q0:, :] + _dot(p, v_ref[rows, cols])
            m_ref[h, q0:, :] = m_new

    def full_tile(j, carry):
        kv_tile(j * tk, tk, 0, False)
        return carry

    lax.fori_loop(0, qi * (tq // tk), full_tile, 0)
    for d in range(tq // td):
        kv_tile(qi * tq + d * td, td, d * td, True)
    even, odd = acc_ref[0], acc_ref[1]
    even = even / even[:, ONE_LANE_EVEN:ONE_LANE_EVEN + 1]
    odd = odd / odd[:, ONE_LANE_ODD:ONE_LANE_ODD + 1]
    lane = lax.broadcasted_iota(I32, even.shape, 1)
    o_ref[...] = jnp.where(lane < V_HEAD, even, odd).astype(BF16)


def attention(q, k, v, batch, seq, to_bf16=(), tq=1024, tk=1024, td=512):
    tq, tk, td = min(tq, seq), min(tk, seq), min(td, seq)
    n_q = seq // tq
    pair = 2 * HEAD_PAD
    n_pairs = N_HEADS // 2
    n_steps = batch * n_pairs * n_q
    flat = [w.reshape(-1, w.shape[-1]) for w in to_bf16]

    def slice_spec(w):
        return pl.BlockSpec((w.shape[0] // n_steps, w.shape[1]), lambda b, hp, i: ((b * n_pairs + hp) * n_q + i, 0))

    o, *cast = pl.pallas_call(
        functools.partial(_attn_kernel, tq=tq, tk=tk, td=td, n_cast=len(flat)),
        out_shape=[jax.ShapeDtypeStruct((batch * seq, N_HEADS * V_HEAD), BF16)]
        + [jax.ShapeDtypeStruct(w.shape, BF16) for w in flat],
        grid=(batch, n_pairs, n_q),
        in_specs=[
            pl.BlockSpec((tq, pair), lambda b, hp, i: (b * n_q + i, hp)),
            pl.BlockSpec((seq, pair), lambda b, hp, i: (b, hp)),
            pl.BlockSpec((seq, pair), lambda b, hp, i: (b, hp)),
        ] + [slice_spec(w) for w in flat],
        out_specs=[pl.BlockSpec((tq, 2 * V_HEAD), lambda b, hp, i: (b * n_q + i, hp))]
        + [slice_spec(w) for w in flat],
        scratch_shapes=[pltpu.VMEM((2, tq, LANES), F32), pltpu.VMEM((2, tq, HEAD_PAD), F32)],
        compiler_params=_params(3, VMEM_LIMIT),
        name="attention",
    )(q, k, v, *flat)
    return o, [c.reshape(w.shape) for c, w in zip(cast, to_bf16)]


ROUTE_COLS = 8
TOKEN_TILE = SUBLANES


def _to_token_tiles(ref, x):
    rows = x.shape[0]
    for k in range(x.shape[1] // LANES):
        ref[pl.ds(k, rows, stride=TOKEN_TILE), :] = x[:, k * LANES:(k + 1) * LANES]


def _from_token_tiles(ref, rows):
    return jnp.concatenate([ref[pl.ds(k, rows, stride=TOKEN_TILE), :] for k in range(TOKEN_TILE)], axis=1)


def _attn_out_router_kernel(a_ref, x_ref, w_ref, g_ref, rw_ref, tri_ref, x3_ref, h_ref, route_t_ref, cnt_ref,
                            carry_ref):
    @pl.when(pl.program_id(0) == 0)
    def _():
        carry_ref[...] = jnp.zeros_like(carry_ref)

    x3 = x_ref[...] + _dot(a_ref[...], w_ref[...])
    x3_ref[...] = x3
    h = _rms(x3, g_ref[...])
    _to_token_tiles(h_ref, h)
    h_hi = h.astype(BF16)
    h_lo = (h - h_hi.astype(F32)).astype(BF16)
    by_hi = _dot(h_hi, rw_ref[...])
    by_lo = _dot(h_lo, rw_ref[...])
    logits = by_hi[:, :N_EXPERTS] + by_hi[:, N_EXPERTS:] + by_lo[:, :N_EXPERTS]
    eye = (lax.broadcasted_iota(I32, (N_EXPERTS, N_EXPERTS), 0)
           == lax.broadcasted_iota(I32, (N_EXPERTS, N_EXPERTS), 1)).astype(F32)
    lt = lax.dot_general(eye, logits, (((1,), (1,)), ((), ())), precision=HIGHEST,
                         preferred_element_type=F32)
    expert = lax.broadcasted_iota(I32, lt.shape, 0)
    m1 = jnp.max(lt, axis=0, keepdims=True)
    i1 = jnp.min(jnp.where(lt == m1, expert, N_EXPERTS), axis=0, keepdims=True)
    rest = jnp.where(expert == i1, -jnp.inf, lt)
    m2 = jnp.max(rest, axis=0, keepdims=True)
    i2 = jnp.min(jnp.where(rest == m2, expert, N_EXPERTS), axis=0, keepdims=True)
    e2 = jnp.exp(m2 - m1)
    w1 = 1.0 / (1.0 + e2)
    w2 = e2 / (1.0 + e2)
    sel = jnp.where((expert == i1) | (expert == i2), 1.0, 0.0)
    incl = _dot(sel.astype(BF16), tri_ref[...])
    carry = carry_ref[:, 0:1]
    excl = incl - sel + carry
    r1 = jnp.sum(jnp.where(expert == i1, excl, 0.0), axis=0, keepdims=True)
    r2 = jnp.sum(jnp.where(expert == i2, excl, 0.0), axis=0, keepdims=True)
    total = carry + incl[:, incl.shape[1] - 1:]
    carry_ref[...] = jnp.broadcast_to(total, carry_ref.shape)
    cnt_ref[...] = jnp.broadcast_to(total, cnt_ref.shape)
    zero = jnp.zeros_like(w1)
    route_t_ref[...] = jnp.concatenate([i1.astype(F32), i2.astype(F32), r1, r2, w1, w2, zero, zero], axis=0)


def attn_out_router(a, x, w_o, g, router_w, rows=1024):
    n, d = x.shape
    rows = min(rows, n)
    tri = (jnp.arange(rows)[:, None] <= jnp.arange(rows)[None, :]).astype(BF16)
    rw_hi = router_w.astype(BF16)
    rw_lo = (router_w - rw_hi.astype(F32)).astype(BF16)
    router_w = jnp.concatenate([rw_hi, rw_lo], axis=1)
    return pl.pallas_call(
        _attn_out_router_kernel,
        out_shape=[jax.ShapeDtypeStruct((n, d), F32), jax.ShapeDtypeStruct((n * TOKEN_TILE, LANES), F32),
                   jax.ShapeDtypeStruct((ROUTE_COLS, n), F32), jax.ShapeDtypeStruct((N_EXPERTS, LANES), F32)],
        grid=(n // rows,),
        in_specs=[
            pl.BlockSpec((rows, a.shape[1]), lambda i: (i, 0)),
            pl.BlockSpec((rows, d), lambda i: (i, 0)),
            pl.BlockSpec(w_o.shape, lambda i: (0, 0)),
            pl.BlockSpec((1, d), lambda i: (0, 0)),
            pl.BlockSpec((d, 2 * N_EXPERTS), lambda i: (0, 0)),
            pl.BlockSpec((rows, rows), lambda i: (0, 0)),
        ],
        out_specs=[
            pl.BlockSpec((rows, d), lambda i: (i, 0)),
            pl.BlockSpec((rows * TOKEN_TILE, LANES), lambda i: (i, 0)),
            pl.BlockSpec((ROUTE_COLS, rows), lambda i: (0, i)),
            pl.BlockSpec((N_EXPERTS, LANES), lambda i: (0, 0)),
        ],
        scratch_shapes=[pltpu.VMEM((N_EXPERTS, LANES), F32)],
        compiler_params=_params(1, VMEM_LIMIT),
        name="attn_out_router",
    )(a, x, w_o, g.reshape(1, d), router_w, tri)


def _tile_copy(src_ref, src_token, dst_ref, dst_token, sem):
    src = src_ref.at[pl.ds(pl.multiple_of(src_token * TOKEN_TILE, TOKEN_TILE), TOKEN_TILE)]
    dst = dst_ref.at[pl.ds(pl.multiple_of(dst_token * TOKEN_TILE, TOKEN_TILE), TOKEN_TILE)]
    return pltpu.make_async_copy(src, dst, sem)


ISSUE_UNROLL = 8


def _dispatch_kernel(zoff_ref, nt_ref, p0_ref, p1_ref, h_ref, xs_ref, zero_ref, sem, zsem, *,
                     rows, tile_rows, n_buf_tiles):
    tile_words = tile_rows * TOKEN_TILE

    @pl.when(pl.program_id(0) == 0)
    def _():
        zero_ref[...] = jnp.zeros_like(zero_ref)
        copies = [pltpu.make_async_copy(
            zero_ref, xs_ref.at[pl.ds(pl.multiple_of(zoff_ref[e] * TOKEN_TILE, TOKEN_TILE), tile_words)], zsem)
            for e in range(N_EXPERTS)]
        for c in copies:
            c.start()
        for c in copies:
            c.wait()
        def fill_idle(t, carry):
            c = pltpu.make_async_copy(
                zero_ref, xs_ref.at[pl.ds(pl.multiple_of(t * tile_words, tile_words), tile_words)], zsem)
            c.start()
            c.wait()
            return carry

        lax.fori_loop(nt_ref[0], n_buf_tiles, fill_idle, 0)

    def issue(r, carry):
        _tile_copy(h_ref, r, xs_ref, p0_ref[r], sem).start(priority=0)
        _tile_copy(h_ref, r, xs_ref, p1_ref[r], sem).start(priority=1)
        return carry

    lax.fori_loop(0, rows, issue, 0, unroll=ISSUE_UNROLL)
    for _ in range(2):
        pltpu.make_async_copy(h_ref, xs_ref.at[pl.ds(0, rows * TOKEN_TILE)], sem).wait()


def dispatch(h, pos0, pos1, zero_off, n_tiles, n_buf_tiles, tile_rows, rows=1024):
    n = h.shape[0] // TOKEN_TILE
    rows = min(rows, n)
    n_steps = n // rows
    grid_spec = pltpu.PrefetchScalarGridSpec(
        num_scalar_prefetch=2,
        grid=(n_steps,),
        in_specs=[
            pl.BlockSpec((rows,), lambda i, z, t: (i,), memory_space=pltpu.SMEM),
            pl.BlockSpec((rows,), lambda i, z, t: (i,), memory_space=pltpu.SMEM),
            pl.BlockSpec((rows * TOKEN_TILE, LANES), lambda i, z, t: (i, 0)),
        ],
        out_specs=pl.BlockSpec(memory_space=pl.ANY),
        scratch_shapes=[pltpu.VMEM((tile_rows * TOKEN_TILE, LANES), F32), pltpu.SemaphoreType.DMA,
                        pltpu.SemaphoreType.DMA],
    )
    return pl.pallas_call(
        functools.partial(_dispatch_kernel, rows=rows, tile_rows=tile_rows, n_buf_tiles=n_buf_tiles),
        out_shape=jax.ShapeDtypeStruct((n_buf_tiles * tile_rows * TOKEN_TILE, LANES), F32),
        grid_spec=grid_spec,
        compiler_params=_params(1, VMEM_LIMIT),
        name="dispatch",
    )(zero_off, n_tiles, pos0, pos1, h)


def _combine_kernel(p0_ref, p1_ref, q0_ref, q1_ref, ys_ref, x_ref, w_ref, g_ref, o_ref, buf, sem, *, rows):
    i = pl.program_id(0)
    slot = i % 2

    def issue(a_ref, b_ref, to_slot):
        def body(r, carry):
            _tile_copy(ys_ref, a_ref[r], buf.at[to_slot, 0], r, sem.at[to_slot]).start(priority=0)
            _tile_copy(ys_ref, b_ref[r], buf.at[to_slot, 1], r, sem.at[to_slot]).start(priority=1)
            return carry

        lax.fori_loop(0, rows, body, 0, unroll=ISSUE_UNROLL)

    @pl.when(i == 0)
    def _():
        issue(p0_ref, p1_ref, 0)

    @pl.when(i + 1 < pl.num_programs(0))
    def _():
        issue(q0_ref, q1_ref, 1 - slot)

    for k in range(2):
        pltpu.make_async_copy(ys_ref.at[pl.ds(0, rows * TOKEN_TILE)], buf.at[slot, k], sem.at[slot]).wait()
    diag = (lax.broadcasted_iota(I32, (rows, rows), 0) == lax.broadcasted_iota(I32, (rows, rows), 1))

    def column(v):
        return jnp.sum(jnp.where(diag, jnp.broadcast_to(v, (rows, rows)), 0.0), axis=1, keepdims=True)

    y = (x_ref[...] + column(w_ref[4:5, :]) * _from_token_tiles(buf.at[slot, 0], rows)
         + column(w_ref[5:6, :]) * _from_token_tiles(buf.at[slot, 1], rows))
    o_ref[...] = _rms(y, g_ref[...])


def combine(ys, pos0, pos1, x, route_t, g, rows=512):
    n, d = x.shape
    rows = min(rows, n)
    n_blocks = n // rows

    def cur(i):
        return (i,)

    def nxt(i):
        return (jnp.minimum(i + 1, n_blocks - 1),)

    return pl.pallas_call(
        functools.partial(_combine_kernel, rows=rows),
        out_shape=jax.ShapeDtypeStruct((n, d), F32),
        grid=(n_blocks,),
        in_specs=[
            pl.BlockSpec((rows,), cur, memory_space=pltpu.SMEM),
            pl.BlockSpec((rows,), cur, memory_space=pltpu.SMEM),
            pl.BlockSpec((rows,), nxt, memory_space=pltpu.SMEM),
            pl.BlockSpec((rows,), nxt, memory_space=pltpu.SMEM),
            pl.BlockSpec(memory_space=pl.ANY),
            pl.BlockSpec((rows, d), lambda i: (i, 0)),
            pl.BlockSpec((ROUTE_COLS, rows), lambda i: (0, i)),
            pl.BlockSpec((1, d), lambda i: (0, 0)),
        ],
        out_specs=pl.BlockSpec((rows, d), lambda i: (i, 0)),
        scratch_shapes=[pltpu.VMEM((2, 2, rows * TOKEN_TILE, LANES), F32), pltpu.SemaphoreType.DMA((2,))],
        compiler_params=_params(1, VMEM_LIMIT),
        name="combine",
    )(pos0, pos1, pos0, pos1, ys, x, route_t, g.reshape(1, d))


def moe_layer(attn, x2, w_o, g_ffn, g_final, router_w, wg, wu, wd, tile_rows=512):
    n, d = x2.shape
    tile_rows = min(tile_rows, n)
    x3, h, route_t, counts = attn_out_router(attn, x2, w_o, g_ffn, router_w)
    counts = counts[:, 0].astype(I32)
    tiles = (counts + tile_rows - 1) // tile_rows
    tile_start = jnp.cumsum(tiles) - tiles
    row_start = tile_start * tile_rows
    n_tiles = jnp.sum(tiles).reshape(1).astype(I32)
    max_tiles = (2 * n) // tile_rows + N_EXPERTS
    n_buf_tiles = max_tiles + 1
    e0, e1, rank0, rank1 = (route_t[c].astype(I32) for c in range(4))
    experts = jnp.arange(N_EXPERTS, dtype=I32)[:, None]

    def start_of(e):
        return jnp.sum(jnp.where(e[None, :] == experts, row_start[:, None], 0), axis=0)

    pos0 = start_of(e0) + rank0
    pos1 = start_of(e1) + rank1
    t_idx = jnp.arange(max_tiles, dtype=I32)
    tile_expert = jnp.sum(t_idx[:, None] >= (tile_start + tiles)[None, :], axis=1).astype(I32)
    tile_expert = jnp.where(t_idx < n_tiles[0], jnp.minimum(tile_expert, N_EXPERTS - 1),
                            tile_expert[n_tiles[0] - 1])
    zero_off = (row_start + counts).astype(I32)
    xs = dispatch(h, pos0, pos1, zero_off, n_tiles, n_buf_tiles, tile_rows)
    ys = moe_experts(xs, tile_expert, n_tiles, wg, wu, wd, max_tiles, tile_rows)
    return combine(ys, pos0, pos1, x3, route_t, g_final)


def kernel(x, positions, norm_mix, norm_ffn, final_norm, s5_w_in, s5_lambda_re, s5_lambda_im, s5_log_dt, s5_b_re, s5_b_im, s5_c_re, s5_c_im, s5_d, s5_w_glu, s5_w_out, kv_norm, w_dkv, kv_latent_norm, w_ukv, w_dq, q_latent_norm, w_uq, w_o, ffn_w_gate, ffn_w_up, ffn_w_down, router_w, moe_w_gate, moe_w_up, moe_w_down):
    batch, seq, d = x.shape
    n = batch * seq
    x0 = x.reshape(n, d)
    u = norm_matmul_chunked(x0, norm_mix[0], s5_w_in[0].astype(BF16), BF16, S5_T)
    s5_w = _s5_weights(s5_lambda_re[0], s5_lambda_im[0], s5_log_dt[0], s5_b_re[0], s5_b_im[0],
                       s5_c_re[0], s5_c_im[0])
    y = s5_scan(u, *s5_w, s5_d[0], batch, seq).reshape(n, d)
    x2 = s5_out_ffn(y, x0, s5_w_glu[0].astype(BF16), s5_w_out[0].astype(BF16), norm_ffn[0],
                    ffn_w_gate[0].astype(BF16), ffn_w_up[0].astype(BF16), ffn_w_down[0].astype(BF16))
    cos_tab, sin_tab = rope_tables(positions.reshape(n))
    mla_w = _mla_weights(w_dkv, w_ukv, w_uq[0])
    q, k, v = qkv_project(x2, kv_norm, norm_mix[1], kv_latent_norm, q_latent_norm[0], w_dq[0].astype(BF16),
                          mla_w, cos_tab, sin_tab)
    o, moe_w = attention(q, k, v, batch, seq, to_bf16=(moe_w_gate[0], moe_w_up[0], moe_w_down[0]))
    out = moe_layer(o, x2, w_o[0].astype(BF16), norm_ffn[1], final_norm, router_w[0], *moe_w)
    return out.reshape(batch, seq, d)
```
